```python
import jax, jax.numpy as jnp
from jax import lax
import numpy as np

D_MODEL = 2048
BATCH = 2
SEQ = 4096
DEPTH = 2
DEC_BATCH = 128
DEC_SEQ = 4
PAST_LEN = 8192
PAGE_SIZE = 128

CONV_CH = 1024
CONV_WIDTH = 31
MLA_HEADS = 8
Q_LORA = 512
KV_LORA = 512
NOPE_DIM = 128
ROPE_DIM = 64
V_DIM = 128
ROPE_THETA = 10000.0
Q_BLOCK = 128
GDN_HEADS = 8
GDN_DK = 128
GDN_DV = 128
GDN_QK = GDN_HEADS * GDN_DK
GDN_QKV = 2 * GDN_HEADS * GDN_DK + GDN_HEADS * GDN_DV
SHORT_CONV = 4
GDN_CHUNK = 64
N_GROUPS = 4
EXPERTS_PER_GROUP = 8
N_EXPERTS = N_GROUPS * EXPERTS_PER_GROUP
TOP_K_IN_GROUP = 2
EXPERT_FF = 512
N_BRANCHES = 3
IN_SPLITS = (2 * CONV_CH, Q_LORA, KV_LORA, ROPE_DIM, GDN_QKV, GDN_HEADS * GDN_DV, GDN_HEADS, GDN_HEADS, N_BRANCHES * D_MODEL)
D_IN = sum(IN_SPLITS)
DEEPNORM_ALPHA = (2.0 * DEPTH) ** 0.25
DEEPNORM_BETA = (8.0 * DEPTH) ** -0.25
EPS = 1e-6

kernel_name = 'hybrid_conv_mla_gdn_hmoe_step'


def layer_norm(x, g, b):
    xf = x.astype(jnp.float32)
    mu = jnp.mean(xf, -1, keepdims=True)
    var = jnp.mean(jnp.square(xf - mu), -1, keepdims=True)
    return ((xf - mu) * lax.rsqrt(var + EPS) * g + b).astype(x.dtype)


def rms_norm(x, g):
    xf = x.astype(jnp.float32)
    return (xf * lax.rsqrt(jnp.mean(jnp.square(xf), -1, keepdims=True) + EPS) * g).astype(x.dtype)


def l2_normalize(x):
    return x * lax.rsqrt(jnp.sum(jnp.square(x), -1, keepdims=True) + EPS)


def rotary(x, pos):
    half = ROPE_DIM // 2
    inv_freq = ROPE_THETA ** (-jnp.arange(half, dtype=jnp.float32) / half)
    ang = pos.astype(jnp.float32)[:, None] * inv_freq[None, :]
    shape = (pos.shape[0],) + (1,) * (x.ndim - 3) + (half,)
    cos = jnp.cos(ang).reshape(shape)
    sin = jnp.sin(ang).reshape(shape)
    x1 = x[..., :half].astype(jnp.float32)
    x2 = x[..., half:].astype(jnp.float32)
    return jnp.concatenate([x1 * cos - x2 * sin, x2 * cos + x1 * sin], -1).astype(x.dtype)


def causal_depthwise_conv(u, buf, w):
    xp = jnp.concatenate([buf, u], axis=1)
    y = lax.conv_general_dilated(xp, w[:, None, :], window_strides=(1,), padding='VALID',
                                 dimension_numbers=('NWC', 'WIO', 'NWC'), feature_group_count=u.shape[-1])
    return y, xp[:, -(w.shape[0] - 1):]


def conformer_conv_branch(glu_in, buf, dw_w, dw_b, ln_g, ln_b, w_pw):
    a, gate = jnp.split(glu_in, 2, axis=-1)
    u = a * jax.nn.sigmoid(gate)
    y, new_buf = causal_depthwise_conv(u, buf, dw_w)
    y = jax.nn.silu(layer_norm(y + dw_b, ln_g, ln_b))
    return y @ w_pw, new_buf


def mla_prompt(q_nope, q_rope, c_kv, k_rope, pos, w_uk, w_uv):
    B, T, H, _ = q_nope.shape
    scale = (NOPE_DIM + ROPE_DIM) ** -0.5
    k_nope = jnp.einsum('btc,chn->bthn', c_kv, w_uk)
    v = jnp.einsum('btc,chv->bthv', c_kv, w_uv)
    nb = T // Q_BLOCK

    def block(args):
        qn, qr, qpos = args
        s = (jnp.einsum('bqhn,bkhn->bhqk', qn, k_nope, preferred_element_type=jnp.float32)
             + jnp.einsum('bqhr,bkr->bhqk', qr, k_rope, preferred_element_type=jnp.float32)) * scale
        s = jnp.where(pos[None, :] <= qpos[:, None], s, -jnp.inf)
        p = jax.nn.softmax(s, axis=-1).astype(v.dtype)
        return jnp.einsum('bhqk,bkhv->bqhv', p, v)

    qn_b = jnp.moveaxis(q_nope.reshape(B, nb, Q_BLOCK, H, NOPE_DIM), 1, 0)
    qr_b = jnp.moveaxis(q_rope.reshape(B, nb, Q_BLOCK, H, ROPE_DIM), 1, 0)
    out = lax.map(block, (qn_b, qr_b, pos.reshape(nb, Q_BLOCK)))
    return jnp.moveaxis(out, 0, 1).reshape(B, T, H * V_DIM)


def mla_sample(q_nope, q_rope, c_kv, k_rope, ckv_past, kr_past, w_uk, w_uv):
    B, T, H, _ = q_nope.shape
    P = ckv_past.shape[1]
    scale = (NOPE_DIM + ROPE_DIM) ** -0.5
    q_lat = jnp.einsum('bthn,chn->bthc', q_nope, w_uk)
    s_past = (jnp.einsum('bthc,bpc->bhtp', q_lat, ckv_past, preferred_element_type=jnp.float32)
              + jnp.einsum('bthr,bpr->bhtp', q_rope, kr_past, preferred_element_type=jnp.float32)) * scale
    s_new = (jnp.einsum('bthc,bkc->bhtk', q_lat, c_kv, preferred_element_type=jnp.float32)
             + jnp.einsum('bthr,bkr->bhtk', q_rope, k_rope, preferred_element_type=jnp.float32)) * scale
    s_new = jnp.where(jnp.tril(jnp.ones((T, T), bool)), s_new, -jnp.inf)
    p = jax.nn.softmax(jnp.concatenate([s_past, s_new], -1), axis=-1).astype(c_kv.dtype)
    o_lat = (jnp.einsum('bhtp,bpc->bthc', p[..., :P], ckv_past)
             + jnp.einsum('bhtk,bkc->bthc', p[..., P:], c_kv))
    return jnp.einsum('bthc,chv->bthv', o_lat, w_uv).reshape(B, T, H * V_DIM)


def gated_delta_chunked(q, k, v, g, beta, S0):
    B, T, H, DK = q.shape
    DV = v.shape[-1]
    C = GDN_CHUNK if T % GDN_CHUNK == 0 else T
    N = T // C

    def chunks(x):
        return jnp.moveaxis(x.reshape((B, N, C, H) + x.shape[3:]), (1, 3), (0, 2))

    qc, kc, vc, gc, bc = chunks(q), chunks(k), chunks(v), chunks(g), chunks(beta)
    G = jnp.cumsum(gc, axis=-1)
    diff = G[..., :, None] - G[..., None, :]
    strict = jnp.tril(jnp.ones((C, C), bool), -1)
    incl = jnp.tril(jnp.ones((C, C), bool))
    dec_strict = jnp.exp(jnp.where(strict, diff, -jnp.inf))
    dec_incl = jnp.exp(jnp.where(incl, diff, -jnp.inf))
    A = jnp.eye(C, dtype=jnp.float32) + bc[..., :, None] * jnp.einsum('nbhid,nbhjd->nbhij', kc, kc) * dec_strict
    rhs = jnp.concatenate([bc[..., None] * vc, (bc * jnp.exp(G))[..., None] * kc], axis=-1)
    sol = lax.linalg.triangular_solve(A, rhs, left_side=True, lower=True)
    U, W = sol[..., :DV], sol[..., DV:]
    attn = jnp.einsum('nbhid,nbhjd->nbhij', qc, kc) * dec_incl
    q_dec = qc * jnp.exp(G)[..., None]
    G_last = G[..., -1:]
    k_tail = kc * jnp.exp(G_last - G)[..., None]
    gamma = jnp.exp(G_last)[..., None]

    def step(S, inp):
        U_n, W_n, q_n, attn_n, k_n, gam_n = inp
        u = U_n - jnp.einsum('bhcd,bhdv->bhcv', W_n, S)
        o = jnp.einsum('bhcd,bhdv->bhcv', q_n, S) + jnp.einsum('bhij,bhjv->bhiv', attn_n, u)
        S = gam_n * S + jnp.einsum('bhcd,bhcv->bhdv', k_n, u)
        return S, o

    S, o = lax.scan(step, S0, (U, W, q_dec, attn, k_tail, gamma))
    o = jnp.moveaxis(o, (0, 2), (1, 3)).reshape(B, T, H, DV)
    return o, S


def gdn_branch(qkv_raw, z, b_raw, a_raw, conv_buf, S0, conv_w, a_log, dt_bias, norm_g, w_o):
    B, T, _ = qkv_raw.shape
    qkv, new_buf = causal_depthwise_conv(qkv_raw, conv_buf, conv_w)
    qkv = jax.nn.silu(qkv).astype(jnp.float32)
    q, k, v = jnp.split(qkv, [GDN_QK, 2 * GDN_QK], axis=-1)
    q = l2_normalize(q.reshape(B, T, GDN_HEADS, GDN_DK)) * (GDN_DK ** -0.5)
    k = l2_normalize(k.reshape(B, T, GDN_HEADS, GDN_DK))
    v = v.reshape(B, T, GDN_HEADS, GDN_DV)
    beta = jax.nn.sigmoid(b_raw.astype(jnp.float32))
    g = -jnp.exp(a_log.astype(jnp.float32)) * jax.nn.softplus(a_raw.astype(jnp.float32) + dt_bias.astype(jnp.float32))
    o, S = gated_delta_chunked(q, k, v, g, beta, S0.astype(jnp.float32))
    o = rms_norm(o, norm_g) * jax.nn.silu(z.astype(jnp.float32).reshape(B, T, GDN_HEADS, GDN_DV))
    return o.reshape(B, T, GDN_HEADS * GDN_DV).astype(qkv_raw.dtype) @ w_o, new_buf, S.astype(qkv_raw.dtype)


def mixer_sublayer(x, pos, conf_buf, gconv_buf, gdn_state, past_ckv, past_krope, lw):
    (w_in, conv_dw, conv_dw_b, conv_ln_g, conv_ln_b, conv_pw, mla_q_norm, mla_w_uq, mla_kv_norm,
     mla_w_uk, mla_w_uv, mla_w_o, gdn_conv_w, gdn_a_log, gdn_dt_bias, gdn_norm, gdn_w_o, w_out) = lw
    B, T, _ = x.shape
    split_points = np.cumsum(IN_SPLITS)[:-1].tolist()
    glu_in, c_q, c_kv, k_rope, qkv_raw, z, b_raw, a_raw, gate_raw = jnp.split(x @ w_in, split_points, axis=-1)
    y_conv, new_conf_buf = conformer_conv_branch(glu_in, conf_buf, conv_dw, conv_dw_b, conv_ln_g, conv_ln_b, conv_pw)
    q = (rms_norm(c_q, mla_q_norm) @ mla_w_uq).reshape(B, T, MLA_HEADS, NOPE_DIM + ROPE_DIM)
    q_nope, q_rope = q[..., :NOPE_DIM], rotary(q[..., NOPE_DIM:], pos)
    c_kv = rms_norm(c_kv, mla_kv_norm)
    k_rope = rotary(k_rope, pos)
    if past_ckv is None:
        o_mla = mla_prompt(q_nope, q_rope, c_kv, k_rope, pos, mla_w_uk, mla_w_uv)
    else:
        o_mla = mla_sample(q_nope, q_rope, c_kv, k_rope, past_ckv, past_krope, mla_w_uk, mla_w_uv)
    y_mla = o_mla @ mla_w_o
    y_gdn, new_gconv, new_state = gdn_branch(qkv_raw, z, b_raw, a_raw, gconv_buf, gdn_state,
                                             gdn_conv_w, gdn_a_log, gdn_dt_bias, gdn_norm, gdn_w_o)
    gates = jax.nn.sigmoid(gate_raw.reshape(B, T, N_BRANCHES, D_MODEL))
    merged = gates[:, :, 0] * y_conv + gates[:, :, 1] * y_mla + gates[:, :, 2] * y_gdn
    return merged @ w_out, (c_kv, k_rope, new_conf_buf, new_gconv, new_state)


def hier_moe(x, rg_w, rg_b, re_w, re_b, w_gate, w_up, w_down):
    B, T, D = x.shape
    xt = x.reshape(B * T, D)
    n = xt.shape[0]
    pg = jax.nn.softmax((xt @ rg_w).astype(jnp.float32) + rg_b, axis=-1)
    g_val, g_idx = lax.top_k(pg, 1)
    le = ((xt @ re_w).astype(jnp.float32) + re_b).reshape(n, N_GROUPS, EXPERTS_PER_GROUP)
    le_sel = le[jnp.arange(n), g_idx[:, 0]]
    e_val, e_idx = lax.top_k(jax.nn.softmax(le_sel, axis=-1), TOP_K_IN_GROUP)
    e_w = e_val / jnp.sum(e_val, -1, keepdims=True)
    expert_id = g_idx * EXPERTS_PER_GROUP + e_idx
    combine = jnp.sum(jax.nn.one_hot(expert_id, N_EXPERTS, dtype=jnp.float32) * (g_val * e_w)[..., None], axis=1)
    h = jax.nn.silu(jnp.einsum('nd,edf->nef', xt, w_gate)) * jnp.einsum('nd,edf->nef', xt, w_up)
    h = h * combine[..., None].astype(h.dtype)
    return jnp.einsum('nef,efd->nd', h, w_down).reshape(B, T, D)


def setup_inputs(seed: int = 0) -> dict:
    key = jax.random.key(seed)
    ks = iter(jax.random.split(key, 48))
    f32 = jnp.float32

    def nrm(shape, scale):
        return jax.random.normal(next(ks), shape, f32) * scale

    n_pages = PAST_LEN // PAGE_SIZE
    n_pool = (DEC_BATCH * n_pages * 5) // 4
    perm = jax.random.permutation(next(ks), n_pool)
    page_table = perm[:DEC_BATCH * n_pages].reshape(DEC_BATCH, n_pages).astype(jnp.int32)
    dt = jnp.exp(jax.random.uniform(next(ks), (DEPTH, GDN_HEADS), f32, np.log(1e-3), np.log(1e-1)))
    return {
        'x_prompt': nrm((BATCH, SEQ, D_MODEL), 1.0),
        'x_sample': nrm((DEC_BATCH, DEC_SEQ, D_MODEL), 1.0),
        'cache_ckv': nrm((DEPTH, n_pool, PAGE_SIZE, KV_LORA), 1.0),
        'cache_krope': nrm((DEPTH, n_pool, PAGE_SIZE, ROPE_DIM), 1.0),
        'state_conf_conv': nrm((DEPTH, DEC_BATCH, CONV_WIDTH - 1, CONV_CH), 0.5),
        'state_gdn_conv': nrm((DEPTH, DEC_BATCH, SHORT_CONV - 1, GDN_QKV), 1.0),
        'state_gdn': nrm((DEPTH, DEC_BATCH, GDN_HEADS, GDN_DK, GDN_DV), GDN_DK ** -0.5),
        'page_table': page_table,
        'w_in': nrm((DEPTH, D_MODEL, D_IN), D_MODEL ** -0.5),
        'conv_dw': nrm((DEPTH, CONV_WIDTH, CONV_CH), CONV_WIDTH ** -0.5),
        'conv_dw_b': nrm((DEPTH, CONV_CH), 0.01),
        'conv_ln_g': 1.0 + nrm((DEPTH, CONV_CH), 0.01),
        'conv_ln_b': nrm((DEPTH, CONV_CH), 0.01),
        'conv_pw': nrm((DEPTH, CONV_CH, D_MODEL), CONV_CH ** -0.5),
        'mla_q_norm': 1.0 + nrm((DEPTH, Q_LORA), 0.01),
        'mla_w_uq': nrm((DEPTH, Q_LORA, MLA_HEADS * (NOPE_DIM + ROPE_DIM)), Q_LORA ** -0.5),
        'mla_kv_norm': 1.0 + nrm((DEPTH, KV_LORA), 0.01),
        'mla_w_uk': nrm((DEPTH, KV_LORA, MLA_HEADS, NOPE_DIM), KV_LORA ** -0.5),
        'mla_w_uv': nrm((DEPTH, KV_LORA, MLA_HEADS, V_DIM), KV_LORA ** -0.5),
        'mla_w_o': nrm((DEPTH, MLA_HEADS * V_DIM, D_MODEL), (MLA_HEADS * V_DIM) ** -0.5),
        'gdn_conv_w': nrm((DEPTH, SHORT_CONV, GDN_QKV), SHORT_CONV ** -0.5),
        'gdn_a_log': jnp.log(jax.random.uniform(next(ks), (DEPTH, GDN_HEADS), f32, 1.0, 16.0)),
        'gdn_dt_bias': dt + jnp.log(-jnp.expm1(-dt)),
        'gdn_norm': 1.0 + nrm((DEPTH, GDN_DV), 0.01),
        'gdn_w_o': nrm((DEPTH, GDN_HEADS * GDN_DV, D_MODEL), (GDN_HEADS * GDN_DV) ** -0.5),
        'w_out': nrm((DEPTH, D_MODEL, D_MODEL), D_MODEL ** -0.5 * DEEPNORM_BETA),
        'ln1_g': 1.0 + nrm((DEPTH, D_MODEL), 0.01),
        'ln1_b': nrm((DEPTH, D_MODEL), 0.01),
        'router_group_w': nrm((DEPTH, D_MODEL, N_GROUPS), D_MODEL ** -0.5),
        'router_group_b': nrm((DEPTH, N_GROUPS), 0.01),
        'router_expert_w': nrm((DEPTH, D_MODEL, N_EXPERTS), D_MODEL ** -0.5),
        'router_expert_b': nrm((DEPTH, N_EXPERTS), 0.01),
        'moe_w_gate': nrm((DEPTH, N_EXPERTS, D_MODEL, EXPERT_FF), D_MODEL ** -0.5),
        'moe_w_up': nrm((DEPTH, N_EXPERTS, D_MODEL, EXPERT_FF), D_MODEL ** -0.5),
        'moe_w_down': nrm((DEPTH, N_EXPERTS, EXPERT_FF, D_MODEL), EXPERT_FF ** -0.5 * DEEPNORM_BETA),
        'ln2_g': 1.0 + nrm((DEPTH, D_MODEL), 0.01),
        'ln2_b': nrm((DEPTH, D_MODEL), 0.01),
    }


def reference(x_prompt, x_sample, cache_ckv, cache_krope, state_conf_conv, state_gdn_conv, state_gdn, page_table,
              w_in, conv_dw, conv_dw_b, conv_ln_g, conv_ln_b, conv_pw, mla_q_norm, mla_w_uq, mla_kv_norm,
              mla_w_uk, mla_w_uv, mla_w_o, gdn_conv_w, gdn_a_log, gdn_dt_bias, gdn_norm, gdn_w_o, w_out,
              ln1_g, ln1_b, router_group_w, router_group_b, router_expert_w, router_expert_b,
              moe_w_gate, moe_w_up, moe_w_down, ln2_g, ln2_b):
    bp, tp, _ = x_prompt.shape
    bs, ts, _ = x_sample.shape
    pos_p = jnp.arange(tp, dtype=jnp.int32)
    pos_s = PAST_LEN + jnp.arange(ts, dtype=jnp.int32)
    dt = x_prompt.dtype
    xp, xs = x_prompt, x_sample
    ckv_p, kr_p, ckv_s, kr_s = [], [], [], []
    cc_p, cc_s, gc_p, gc_s, gs_p, gs_s = [], [], [], [], [], []
    for l in range(DEPTH):
        lw = (w_in[l], conv_dw[l], conv_dw_b[l], conv_ln_g[l], conv_ln_b[l], conv_pw[l], mla_q_norm[l], mla_w_uq[l],
              mla_kv_norm[l], mla_w_uk[l], mla_w_uv[l], mla_w_o[l], gdn_conv_w[l], gdn_a_log[l], gdn_dt_bias[l],
              gdn_norm[l], gdn_w_o[l], w_out[l])
        mw = (router_group_w[l], router_group_b[l], router_expert_w[l], router_expert_b[l],
              moe_w_gate[l], moe_w_up[l], moe_w_down[l])
        mix_p, st_p = mixer_sublayer(
            xp, pos_p,
            jnp.zeros((bp, CONV_WIDTH - 1, CONV_CH), dt),
            jnp.zeros((bp, SHORT_CONV - 1, GDN_QKV), dt),
            jnp.zeros((bp, GDN_HEADS, GDN_DK, GDN_DV), dt),
            None, None, lw)
        past_ckv = cache_ckv[l, page_table].reshape(bs, -1, KV_LORA)
        past_krope = cache_krope[l, page_table].reshape(bs, -1, ROPE_DIM)
        mix_s, st_s = mixer_sublayer(xs, pos_s, state_conf_conv[l], state_gdn_conv[l], state_gdn[l],
                                     past_ckv, past_krope, lw)
        xp = layer_norm(DEEPNORM_ALPHA * xp + mix_p, ln1_g[l], ln1_b[l])
        xs = layer_norm(DEEPNORM_ALPHA * xs + mix_s, ln1_g[l], ln1_b[l])
        xp = layer_norm(DEEPNORM_ALPHA * xp + hier_moe(xp, *mw), ln2_g[l], ln2_b[l])
        xs = layer_norm(DEEPNORM_ALPHA * xs + hier_moe(xs, *mw), ln2_g[l], ln2_b[l])
        ckv_p.append(st_p[0]); kr_p.append(st_p[1]); cc_p.append(st_p[2]); gc_p.append(st_p[3]); gs_p.append(st_p[4])
        ckv_s.append(st_s[0]); kr_s.append(st_s[1]); cc_s.append(st_s[2]); gc_s.append(st_s[3]); gs_s.append(st_s[4])
    return (xp, xs,
            jnp.stack(ckv_p), jnp.stack(kr_p), jnp.stack(ckv_s), jnp.stack(kr_s),
            jnp.stack(cc_p), jnp.stack(cc_s), jnp.stack(gc_p), jnp.stack(gc_s),
            jnp.stack(gs_p), jnp.stack(gs_s))
```

```python
import functools

import numpy as np
import jax
import jax.numpy as jnp
from jax import lax
from jax.experimental import pallas as pl
from jax.experimental.pallas import tpu as pltpu

F32 = jnp.float32
BF16 = jnp.bfloat16
HIGHEST = lax.Precision.HIGHEST

D_MODEL = 2048
CONV_CH = 1024
CONV_WIDTH = 31
MLA_HEADS = 8
Q_LORA = 512
KV_LORA = 512
NOPE_DIM = 128
ROPE_DIM = 64
V_DIM = 128
ROPE_THETA = 10000.0
GDN_HEADS = 8
GDN_DK = 128
GDN_DV = 128
GDN_QK = GDN_HEADS * GDN_DK
GDN_QKV = 2 * GDN_QK + GDN_HEADS * GDN_DV
SHORT_CONV = 4
GDN_CHUNK = 64
N_GROUPS = 4
EXPERTS_PER_GROUP = 8
N_EXPERTS = N_GROUPS * EXPERTS_PER_GROUP
EXPERT_FF = 512
N_BRANCHES = 3
EPS = 1e-6

LANES = 128
SUBLANES = 8
VMEM_LIMIT_BYTES = 56 * 1024 * 1024

COL_GATE = 0
COL_QKV = 6144
COL_A = 9216
COL_G = 10240
COL_Z = 11264
COL_CQ = 12288
COL_CKV = 12800
COL_KR = 13312
COL_BA = 13440
N_PACK = 13824
TN_IN = 512

HEAD_QK = 256
Q_ABS = 640
MOE_TILE = 256
ROW_TILE = 256


def _cparams(sem):
    return pltpu.CompilerParams(dimension_semantics=sem, vmem_limit_bytes=VMEM_LIMIT_BYTES)


def _dot(a, b, precision=None):
    return jnp.dot(a, b, preferred_element_type=F32, precision=precision)


def _dot_nt(a, b, precision=None):
    return lax.dot_general(a, b, (((1,), (1,)), ((), ())), preferred_element_type=F32, precision=precision)


def _dot_tn(a, b, precision=None):
    return lax.dot_general(a, b, (((0,), (0,)), ((), ())), preferred_element_type=F32, precision=precision)


def _sigmoid(x):
    return 1.0 / (1.0 + jnp.exp(-x))


def _silu(x):
    return x * _sigmoid(x)


def _row_tile(n, cap):
    best = SUBLANES
    for t in range(SUBLANES, cap + 1, SUBLANES):
        if n % t == 0:
            best = t
    return best


def _inproj_kernel(x_ref, w_ref, o_ref, xb_ref):
    @pl.when(pl.program_id(1) == 0)
    def _():
        xb_ref[...] = x_ref[...].astype(BF16)

    o_ref[...] = _dot(xb_ref[...], w_ref[...].astype(BF16))


def _inproj(xt, w_pack):
    n, d = xt.shape
    tm = _row_tile(n, 1100)
    return pl.pallas_call(
        _inproj_kernel,
        grid=(n // tm, N_PACK // TN_IN),
        in_specs=[pl.BlockSpec((tm, d), lambda i, j: (i, 0)),
                  pl.BlockSpec((d, TN_IN), lambda i, j: (0, j))],
        out_specs=pl.BlockSpec((tm, TN_IN), lambda i, j: (i, j)),
        out_shape=jax.ShapeDtypeStruct((n, N_PACK), F32),
        scratch_shapes=[pltpu.VMEM((tm, d), BF16)],
        compiler_params=_cparams(("parallel", "arbitrary")),
        name="inproj",
    )(xt, w_pack)


def _ln_rows(y, g, b):
    mu = jnp.mean(y, axis=-1, keepdims=True)
    yc = y - mu
    var = jnp.mean(yc * yc, axis=-1, keepdims=True)
    return yc * lax.rsqrt(var + EPS) * g + b


CONV_HALO = 32


def _conf_prompt_kernel(a_ref, g_ref, ap_ref, gp_ref, w_ref, b_ref, lg_ref, lb_ref, act_ref, u_ref, xp_ref, y_ref):
    tt = a_ref.shape[0]
    u = a_ref[...] * _sigmoid(g_ref[...])
    u_ref[...] = u
    xp_ref[CONV_HALO:, :] = u
    up = ap_ref[...] * _sigmoid(gp_ref[...])
    xp_ref[0:CONV_HALO, :] = jnp.where(pl.program_id(1) > 0, up, 0.0)
    off = CONV_HALO - (CONV_WIDTH - 1)
    for c in range(CONV_CH // LANES):
        cs = slice(c * LANES, (c + 1) * LANES)
        acc = jnp.zeros((tt, LANES), F32)
        for j in range(CONV_WIDTH):
            acc = acc + xp_ref[off + j:off + j + tt, cs] * w_ref[j:j + 1, cs]
        y_ref[:, cs] = acc + b_ref[:, cs]
    yn = _ln_rows(y_ref[...], lg_ref[...], lb_ref[...])
    act_ref[...] = _silu(yn).astype(BF16)


def _conf_prompt(act, bp, tp, conv_dw, dw_b, ln_g, ln_b):
    tt = ROW_TILE
    nt = tp // tt
    ca, cg = COL_A // CONV_CH, COL_G // CONV_CH
    r = tt // CONV_HALO

    def cur(col):
        return pl.BlockSpec((tt, CONV_CH), lambda b, i: (b * nt + i, col))

    def prev(col):
        return pl.BlockSpec((CONV_HALO, CONV_CH), lambda b, i: (jnp.maximum((b * nt + i) * r - 1, 0), col))

    vec = pl.BlockSpec((1, CONV_CH), lambda b, i: (0, 0))
    return pl.pallas_call(
        _conf_prompt_kernel,
        grid=(bp, nt),
        in_specs=[cur(ca), cur(cg), prev(ca), prev(cg),
                  pl.BlockSpec((CONV_WIDTH, CONV_CH), lambda b, i: (0, 0)), vec, vec, vec],
        out_specs=[pl.BlockSpec((tt, CONV_CH), lambda b, i: (b * nt + i, 0)),
                   pl.BlockSpec((tt, CONV_CH), lambda b, i: (b * nt + i, 0))],
        out_shape=[jax.ShapeDtypeStruct((bp * tp, CONV_CH), BF16),
                   jax.ShapeDtypeStruct((bp * tp, CONV_CH), F32)],
        scratch_shapes=[pltpu.VMEM((tt + CONV_HALO, CONV_CH), F32), pltpu.VMEM((tt, CONV_CH), F32)],
        compiler_params=_cparams(("parallel", "arbitrary")),
        name="conf_conv_prompt",
    )(act, act, act, act, conv_dw, dw_b.reshape(1, -1), ln_g.reshape(1, -1), ln_b.reshape(1, -1))


SAMPLE_BB = 8


def _conf_sample_kernel(ts, a_ref, g_ref, buf_ref, w_ref, b_ref, lg_ref, lb_ref, act_ref, nbuf_ref, xp_ref, y_ref):
    nb = CONV_WIDTH - 1
    u = a_ref[...] * _sigmoid(g_ref[...])
    w = w_ref[...]
    for bi in range(SAMPLE_BB):
        xp_ref[0:nb, :] = buf_ref[bi]
        xp_ref[nb:nb + ts, :] = u[bi * ts:(bi + 1) * ts, :]
        nbuf_ref[bi] = xp_ref[ts:ts + nb, :]
        for t in range(ts):
            y_ref[bi * ts + t:bi * ts + t + 1, :] = jnp.sum(xp_ref[t:t + CONV_WIDTH, :] * w, axis=0, keepdims=True)
    yn = _ln_rows(y_ref[...] + b_ref[...], lg_ref[...], lb_ref[...])
    act_ref[...] = _silu(yn).astype(BF16)


def _conf_sample(act, n_p, bs, ts, buf, conv_dw, dw_b, ln_g, ln_b):
    bb = SAMPLE_BB
    rows = bb * ts
    nb = CONV_WIDTH - 1
    base = n_p // rows
    vec = pl.BlockSpec((1, CONV_CH), lambda i: (0, 0))
    return pl.pallas_call(
        functools.partial(_conf_sample_kernel, ts),
        grid=(bs // bb,),
        in_specs=[pl.BlockSpec((rows, CONV_CH), lambda i: (base + i, COL_A // CONV_CH)),
                  pl.BlockSpec((rows, CONV_CH), lambda i: (base + i, COL_G // CONV_CH)),
                  pl.BlockSpec((bb, nb, CONV_CH), lambda i: (i, 0, 0)),
                  pl.BlockSpec((CONV_WIDTH, CONV_CH), lambda i: (0, 0)), vec, vec, vec],
        out_specs=[pl.BlockSpec((rows, CONV_CH), lambda i: (i, 0)),
                   pl.BlockSpec((bb, nb, CONV_CH), lambda i: (i, 0, 0))],
        out_shape=[jax.ShapeDtypeStruct((bs * ts, CONV_CH), BF16),
                   jax.ShapeDtypeStruct((bs, nb, CONV_CH), F32)],
        scratch_shapes=[pltpu.VMEM((nb + ts + 6, CONV_CH), F32), pltpu.VMEM((rows, CONV_CH), F32)],
        compiler_params=_cparams(("parallel",)),
        name="conf_conv_sample",
    )(act, act, buf, conv_dw, dw_b.reshape(1, -1), ln_g.reshape(1, -1), ln_b.reshape(1, -1))


def _rms_rows(x, g):
    return x * lax.rsqrt(jnp.mean(x * x, axis=-1, keepdims=True) + EPS) * g


def _mla_q_kernel(cq_ref, g_ref, wa_ref, wb_ref, ca_ref, cb_ref, q_ref):
    cqn = _rms_rows(cq_ref[...], g_ref[...]).astype(BF16)
    qa = _dot(cqn, wa_ref[...])
    qb = _dot(cqn, wb_ref[...])
    ca = ca_ref[...]
    cb = cb_ref[...]
    for h in range(MLA_HEADS):
        hs = slice(h * HEAD_QK, (h + 1) * HEAD_QK)
        q_ref[:, hs] = (qa[:, hs] * ca + qb[:, hs] * cb).astype(BF16)


def _mla_q(act, q_norm, w_qa, w_qb, tab_a, tab_b):
    n = act.shape[0]
    tm = ROW_TILE
    nq = MLA_HEADS * HEAD_QK
    return pl.pallas_call(
        _mla_q_kernel,
        grid=(n // tm,),
        in_specs=[pl.BlockSpec((tm, Q_LORA), lambda i: (i, COL_CQ // Q_LORA)),
                  pl.BlockSpec((1, Q_LORA), lambda i: (0, 0)),
                  pl.BlockSpec((Q_LORA, nq), lambda i: (0, 0)),
                  pl.BlockSpec((Q_LORA, nq), lambda i: (0, 0)),
                  pl.BlockSpec((tm, HEAD_QK), lambda i: (i, 0)),
                  pl.BlockSpec((tm, HEAD_QK), lambda i: (i, 0))],
        out_specs=pl.BlockSpec((tm, nq), lambda i: (i, 0)),
        out_shape=jax.ShapeDtypeStruct((n, nq), BF16),
        compiler_params=_cparams(("parallel",)),
        name="mla_q",
    )(act, q_norm.reshape(1, -1), w_qa, w_qb, tab_a, tab_b)


def _mla_kv_kernel(ckv_ref, krr_ref, g_ref, cs_ref, wuk_ref, wuv_ref, e_ref, ckv_out, kr_out, k_out, v_out):
    ckvn = _rms_rows(ckv_ref[...], g_ref[...])
    ckv_out[...] = ckvn
    t = krr_ref[...] * cs_ref[...]
    kr2 = t + pltpu.roll(t, ROPE_DIM, axis=1)
    kr_out[...] = kr2[:, :ROPE_DIM]
    cb = ckvn.astype(BF16)
    k_out[...] = (_dot(cb, wuk_ref[...]) + _dot(kr2.astype(BF16), e_ref[...])).astype(BF16)
    v_out[...] = _dot(cb, wuv_ref[...]).astype(BF16)


def _mla_kv(act, kv_norm, tab_cs, w_uk_pad, w_uv, e_place):
    n = act.shape[0]
    tm = ROW_TILE
    nk = MLA_HEADS * HEAD_QK
    nv = MLA_HEADS * V_DIM
    return pl.pallas_call(
        _mla_kv_kernel,
        grid=(n // tm,),
        in_specs=[pl.BlockSpec((tm, KV_LORA), lambda i: (i, COL_CKV // KV_LORA)),
                  pl.BlockSpec((tm, LANES), lambda i: (i, COL_KR // LANES)),
                  pl.BlockSpec((1, KV_LORA), lambda i: (0, 0)),
                  pl.BlockSpec((tm, LANES), lambda i: (i, 0)),
                  pl.BlockSpec((KV_LORA, nk), lambda i: (0, 0)),
                  pl.BlockSpec((KV_LORA, nv), lambda i: (0, 0)),
                  pl.BlockSpec((LANES, nk), lambda i: (0, 0))],
        out_specs=[pl.BlockSpec((tm, KV_LORA), lambda i: (i, 0)),
                   pl.BlockSpec((tm, ROPE_DIM), lambda i: (i, 0)),
                   pl.BlockSpec((tm, nk), lambda i: (i, 0)),
                   pl.BlockSpec((tm, nv), lambda i: (i, 0))],
        out_shape=[jax.ShapeDtypeStruct((n, KV_LORA), F32),
                   jax.ShapeDtypeStruct((n, ROPE_DIM), F32),
                   jax.ShapeDtypeStruct((n, nk), BF16),
                   jax.ShapeDtypeStruct((n, nv), BF16)],
        compiler_params=_cparams(("parallel",)),
        name="mla_kv",
    )(act, act, kv_norm.reshape(1, -1), tab_cs, w_uk_pad, w_uv, e_place)


ATT_TILE = 512


def _flash_kernel(q_ref, k_ref, v_ref, o_ref, m_ref, l_ref, acc_ref):
    qi = pl.program_id(2)
    ki = pl.program_id(3)

    @pl.when(ki == 0)
    def _():
        m_ref[...] = jnp.full_like(m_ref, -jnp.inf)
        l_ref[...] = jnp.zeros_like(l_ref)
        acc_ref[...] = jnp.zeros_like(acc_ref)

    def update(masked):
        s = _dot_nt(q_ref[...], k_ref[...])
        if masked:
            rows = lax.broadcasted_iota(jnp.int32, s.shape, 0)
            cols = lax.broadcasted_iota(jnp.int32, s.shape, 1)
            s = jnp.where(cols <= rows, s, -jnp.inf)
        m_old = m_ref[...]
        m_new = jnp.maximum(m_old, jnp.max(s, axis=-1, keepdims=True))
        p = jnp.exp(s - m_new)
        alpha = jnp.exp(m_old - m_new)
        l_ref[...] = alpha * l_ref[...] + jnp.sum(p, axis=-1, keepdims=True)
        acc_ref[...] = alpha * acc_ref[...] + _dot(p.astype(BF16), v_ref[...])
        m_ref[...] = m_new

    @pl.when(ki < qi)
    def _():
        update(False)

    @pl.when(ki == qi)
    def _():
        update(True)
        o_ref[...] = (acc_ref[...] / l_ref[...]).astype(BF16)


def _flash_prompt(q, k, v, bp, tp):
    t = ATT_TILE
    nt = tp // t
    return pl.pallas_call(
        _flash_kernel,
        grid=(bp, MLA_HEADS, nt, nt),
        in_specs=[pl.BlockSpec((t, HEAD_QK), lambda b, h, qi, ki: (b * nt + qi, h)),
                  pl.BlockSpec((t, HEAD_QK), lambda b, h, qi, ki: (b * nt + jnp.minimum(ki, qi), h)),
                  pl.BlockSpec((t, V_DIM), lambda b, h, qi, ki: (b * nt + jnp.minimum(ki, qi), h))],
        out_specs=pl.BlockSpec((t, V_DIM), lambda b, h, qi, ki: (b * nt + qi, h)),
        out_shape=jax.ShapeDtypeStruct((bp * tp, MLA_HEADS * V_DIM), BF16),
        scratch_shapes=[pltpu.VMEM((t, 1), F32), pltpu.VMEM((t, 1), F32), pltpu.VMEM((t, V_DIM), F32)],
        compiler_params=_cparams(("parallel", "parallel", "parallel", "arbitrary")),
        name="mla_flash_prompt",
    )(q, k, v)


def _q_absorb_kernel(q_ref, wt_ref, o_ref):
    q = q_ref[...]
    o_ref[:, :KV_LORA] = _dot(q[:, :NOPE_DIM], wt_ref[...]).astype(BF16)
    o_ref[:, KV_LORA:] = q[:, NOPE_DIM:]


def _q_absorb(q_s, w_uk_t):
    n_s = q_s.shape[0]
    return pl.pallas_call(
        _q_absorb_kernel,
        grid=(MLA_HEADS,),
        in_specs=[pl.BlockSpec((n_s, HEAD_QK), lambda h: (0, h)),
                  pl.BlockSpec((None, NOPE_DIM, KV_LORA), lambda h: (h, 0, 0))],
        out_specs=pl.BlockSpec((n_s, Q_ABS), lambda h: (0, h)),
        out_shape=jax.ShapeDtypeStruct((n_s, MLA_HEADS * Q_ABS), BF16),
        compiler_params=_cparams(("parallel",)),
        name="mla_q_absorb",
    )(q_s, w_uk_t)


PAGES_PER_STEP = 8
NEW_PAD = 8


def _paged_kernel(ppb, page, ts, pt_ref, q_ref, cn_ref, kn_ref, *refs):
    ck_refs = refs[:ppb]
    kr_refs = refs[ppb:2 * ppb]
    o_ref = refs[2 * ppb]
    kc_s, kr_s, nc_s, nr_s, m_ref, l_ref, acc_ref = refs[2 * ppb + 1:]
    s_idx = pl.program_id(1)
    rows = q_ref.shape[1]

    @pl.when(s_idx == 0)
    def _():
        m_ref[...] = jnp.full_like(m_ref, -jnp.inf)
        l_ref[...] = jnp.zeros_like(l_ref)
        acc_ref[...] = jnp.zeros_like(acc_ref)

    for k in range(ppb):
        kc_s[k * page:(k + 1) * page, :] = ck_refs[k][...].astype(BF16)
        kr_s[k * page:(k + 1) * page, :] = kr_refs[k][...].astype(BF16)
    q = q_ref[0]
    q_lat = q[:, :KV_LORA]
    q_rope = q[:, KV_LORA:KV_LORA + ROPE_DIM]

    def update(s, vals):
        m_old = m_ref[...]
        m_new = jnp.maximum(m_old, jnp.max(s, axis=-1, keepdims=True))
        p = jnp.exp(s - m_new)
        alpha = jnp.exp(m_old - m_new)
        l_ref[...] = alpha * l_ref[...] + jnp.sum(p, axis=-1, keepdims=True)
        acc_ref[...] = alpha * acc_ref[...] + _dot(p.astype(BF16), vals)
        m_ref[...] = m_new

    update(_dot_nt(q_lat, kc_s[...]) + _dot_nt(q_rope, kr_s[...]), kc_s[...])

    @pl.when(s_idx == pl.num_programs(1) - 1)
    def _():
        nc_s[...] = jnp.zeros_like(nc_s)
        nr_s[...] = jnp.zeros_like(nr_s)
        nc_s[0:NEW_PAD, :] = cn_ref[0].astype(BF16)
        nr_s[0:NEW_PAD, :] = kn_ref[0].astype(BF16)
        s = _dot_nt(q_lat, nc_s[...]) + _dot_nt(q_rope, nr_s[...])
        tok = lax.broadcasted_iota(jnp.int32, s.shape, 0) // MLA_HEADS
        col = lax.broadcasted_iota(jnp.int32, s.shape, 1)
        s = jnp.where((col <= tok) & (col < ts), s, -jnp.inf)
        update(s, nc_s[...])
        o_ref[0] = (acc_ref[...] / l_ref[...]).astype(BF16)


def _paged_attention(layer, page_table, q_abs, ckv_new, kr_new, cache_ckv, cache_krope, ts):
    bs, n_pages = page_table.shape
    page = cache_ckv.shape[2]
    ppb = PAGES_PER_STEP
    rows = ts * MLA_HEADS

    def ck_spec(k):
        return pl.BlockSpec((None, None, page, KV_LORA), lambda b, s, pt: (layer, pt[b, s * ppb + k], 0, 0))

    def kr_spec(k):
        return pl.BlockSpec((None, None, page, ROPE_DIM), lambda b, s, pt: (layer, pt[b, s * ppb + k], 0, 0))

    grid_spec = pltpu.PrefetchScalarGridSpec(
        num_scalar_prefetch=1,
        grid=(bs, n_pages // ppb),
        in_specs=[pl.BlockSpec((1, rows, Q_ABS), lambda b, s, pt: (b, 0, 0)),
                  pl.BlockSpec((1, NEW_PAD, KV_LORA), lambda b, s, pt: (b, 0, 0)),
                  pl.BlockSpec((1, NEW_PAD, ROPE_DIM), lambda b, s, pt: (b, 0, 0))]
        + [ck_spec(k) for k in range(ppb)] + [kr_spec(k) for k in range(ppb)],
        out_specs=pl.BlockSpec((1, rows, KV_LORA), lambda b, s, pt: (b, 0, 0)),
        scratch_shapes=[pltpu.VMEM((ppb * page, KV_LORA), BF16), pltpu.VMEM((ppb * page, ROPE_DIM), BF16),
                        pltpu.VMEM((LANES, KV_LORA), BF16), pltpu.VMEM((LANES, ROPE_DIM), BF16),
                        pltpu.VMEM((rows, 1), F32), pltpu.VMEM((rows, 1), F32), pltpu.VMEM((rows, KV_LORA), F32)],
    )
    return pl.pallas_call(
        functools.partial(_paged_kernel, ppb, page, ts),
        grid_spec=grid_spec,
        out_shape=jax.ShapeDtypeStruct((bs, rows, KV_LORA), BF16),
        compiler_params=_cparams(("parallel", "arbitrary")),
        name="mla_paged_sample",
    )(page_table, q_abs, ckv_new, kr_new, *([cache_ckv] * ppb), *([cache_krope] * ppb))


def _uv_kernel(o_ref, w_ref, out_ref):
    out_ref[...] = _dot(o_ref[...], w_ref[...]).astype(BF16)


def _value_up(o_lat, w_uv):
    n_s = o_lat.shape[0]
    return pl.pallas_call(
        _uv_kernel,
        grid=(MLA_HEADS,),
        in_specs=[pl.BlockSpec((n_s, KV_LORA), lambda h: (0, h)),
                  pl.BlockSpec((KV_LORA, V_DIM), lambda h: (0, h))],
        out_specs=pl.BlockSpec((n_s, V_DIM), lambda h: (0, h)),
        out_shape=jax.ShapeDtypeStruct((n_s, MLA_HEADS * V_DIM), BF16),
        compiler_params=_cparams(("parallel",)),
        name="mla_value_up",
    )(o_lat, w_uv)


def _gdn_post(y, ba, al, q_ref, k_ref, v_ref, gb_ref):
    y = _silu(y)
    for h in range(GDN_HEADS):
        hs = slice(h * GDN_DK, (h + 1) * GDN_DK)
        qh = y[:, hs]
        kh = y[:, GDN_QK + h * GDN_DK:GDN_QK + (h + 1) * GDN_DK]
        q_ref[:, hs] = qh * lax.rsqrt(jnp.sum(qh * qh, axis=-1, keepdims=True) + EPS) * (GDN_DK ** -0.5)
        k_ref[:, hs] = kh * lax.rsqrt(jnp.sum(kh * kh, axis=-1, keepdims=True) + EPS)
    v_ref[...] = y[:, 2 * GDN_QK:]
    xa = ba + al[1:2, :]
    softplus = jnp.maximum(xa, 0.0) + jnp.log(1.0 + jnp.exp(-jnp.abs(xa)))
    lane = lax.broadcasted_iota(jnp.int32, ba.shape, 1)
    gb_ref[...] = jnp.where(lane < GDN_HEADS, _sigmoid(ba), -jnp.exp(al[0:1, :]) * softplus)


def _gdn_prep_prompt_kernel(x_ref, xp_ref, w_ref, ba_ref, al_ref, q_ref, k_ref, v_ref, gb_ref, s_ref):
    tt = x_ref.shape[0]
    s_ref[SUBLANES:, :] = x_ref[...]
    s_ref[0:SUBLANES, :] = jnp.where(pl.program_id(1) > 0, xp_ref[...], 0.0)
    off = SUBLANES - (SHORT_CONV - 1)
    y = jnp.zeros((tt, GDN_QKV), F32)
    for j in range(SHORT_CONV):
        y = y + s_ref[off + j:off + j + tt, :] * w_ref[j:j + 1, :]
    _gdn_post(y, ba_ref[...], al_ref[...], q_ref, k_ref, v_ref, gb_ref)


def _gdn_prep_prompt(act, bp, tp, conv_w, al):
    tt = 128
    nt = tp // tt
    r = tt // SUBLANES
    cq = COL_QKV // GDN_QKV
    row = lambda b, i: (b * nt + i, 0)
    return pl.pallas_call(
        _gdn_prep_prompt_kernel,
        grid=(bp, nt),
        in_specs=[pl.BlockSpec((tt, GDN_QKV), lambda b, i: (b * nt + i, cq)),
                  pl.BlockSpec((SUBLANES, GDN_QKV), lambda b, i: (jnp.maximum((b * nt + i) * r - 1, 0), cq)),
                  pl.BlockSpec((SHORT_CONV, GDN_QKV), lambda b, i: (0, 0)),
                  pl.BlockSpec((tt, LANES), lambda b, i: (b * nt + i, COL_BA // LANES)),
                  pl.BlockSpec((2, LANES), lambda b, i: (0, 0))],
        out_specs=[pl.BlockSpec((tt, GDN_QK), row), pl.BlockSpec((tt, GDN_QK), row),
                   pl.BlockSpec((tt, GDN_QK), row), pl.BlockSpec((tt, LANES), row)],
        out_shape=[jax.ShapeDtypeStruct((bp * tp, GDN_QK), F32)] * 3 + [jax.ShapeDtypeStruct((bp * tp, LANES), F32)],
        scratch_shapes=[pltpu.VMEM((tt + SUBLANES, GDN_QKV), F32)],
        compiler_params=_cparams(("parallel", "arbitrary")),
        name="gdn_prep_prompt",
    )(act, act, conv_w, act, al)


def _gdn_prep_sample_kernel(ts, x_ref, buf_ref, w_ref, ba_ref, al_ref, q_ref, k_ref, v_ref, gb_ref, s_ref, y_ref, g_ref):
    nb = SHORT_CONV - 1
    y_ref[...] = jnp.zeros_like(y_ref)
    g_ref[...] = jnp.zeros_like(g_ref)
    x = x_ref[...]
    ba = ba_ref[...]
    for bi in range(SAMPLE_BB):
        s_ref[0:nb, :] = buf_ref[bi]
        s_ref[nb:nb + ts, :] = x[bi * ts:(bi + 1) * ts, :]
        y = jnp.zeros((ts, GDN_QKV), F32)
        for j in range(SHORT_CONV):
            y = y + s_ref[j:j + ts, :] * w_ref[j:j + 1, :]
        y_ref[bi * NEW_PAD:bi * NEW_PAD + ts, :] = y
        g_ref[bi * NEW_PAD:bi * NEW_PAD + ts, :] = ba[bi * ts:(bi + 1) * ts, :]
    _gdn_post(y_ref[...], g_ref[...], al_ref[...], q_ref, k_ref, v_ref, gb_ref)
    rowi = lax.broadcasted_iota(jnp.int32, gb_ref.shape, 0) % NEW_PAD
    gb_ref[...] = jnp.where(rowi < ts, gb_ref[...], 0.0)


def _gdn_prep_sample(act, n_p, bs, ts, buf, conv_w, al):
    bb = SAMPLE_BB
    rows = bb * ts
    prow = bb * NEW_PAD
    base = n_p // rows
    nb = SHORT_CONV - 1
    row = lambda i: (i, 0)
    return pl.pallas_call(
        functools.partial(_gdn_prep_sample_kernel, ts),
        grid=(bs // bb,),
        in_specs=[pl.BlockSpec((rows, GDN_QKV), lambda i: (base + i, COL_QKV // GDN_QKV)),
                  pl.BlockSpec((bb, nb, GDN_QKV), lambda i: (i, 0, 0)),
                  pl.BlockSpec((SHORT_CONV, GDN_QKV), lambda i: (0, 0)),
                  pl.BlockSpec((rows, LANES), lambda i: (base + i, COL_BA // LANES)),
                  pl.BlockSpec((2, LANES), lambda i: (0, 0))],
        out_specs=[pl.BlockSpec((prow, GDN_QK), row), pl.BlockSpec((prow, GDN_QK), row),
                   pl.BlockSpec((prow, GDN_QK), row), pl.BlockSpec((prow, LANES), row)],
        out_shape=[jax.ShapeDtypeStruct((bs * NEW_PAD, GDN_QK), F32)] * 3
        + [jax.ShapeDtypeStruct((bs * NEW_PAD, LANES), F32)],
        scratch_shapes=[pltpu.VMEM((2 * SUBLANES, GDN_QKV), F32), pltpu.VMEM((prow, GDN_QKV), F32),
                        pltpu.VMEM((prow, LANES), F32)],
        compiler_params=_cparams(("parallel",)),
        name="gdn_prep_sample",
    )(act, buf, conv_w, act, al)


def _gdn_chunk_kernel(q_ref, k_ref, v_ref, gb_ref, z_ref, s0_ref, ng_ref, o_ref, s_ref):
    c = q_ref.shape[0]

    @pl.when(pl.program_id(1) == 0)
    def _():
        s_ref[...] = s0_ref[...]

    ri = lax.broadcasted_iota(jnp.int32, (c, c), 0)
    ci = lax.broadcasted_iota(jnp.int32, (c, c), 1)
    ltri = (ci <= ri).astype(F32)
    eye = (ci == ri).astype(F32)
    gb = gb_ref[...]
    g_cum = _dot(ltri, gb, HIGHEST)
    n_double = max(int(np.ceil(np.log2(c))) - 1, 0)
    for h in range(GDN_HEADS):
        hs = slice(h * GDN_DK, (h + 1) * GDN_DK)
        q = q_ref[:, hs]
        k = k_ref[:, hs]
        v = v_ref[:, hs]
        beta = gb[:, h:h + 1]
        g = gb[:, GDN_HEADS + h:GDN_HEADS + h + 1]
        gc = g_cum[:, GDN_HEADS + h:GDN_HEADS + h + 1]
        g_last = gc[c - 1:c, :]
        diff = _dot(ltri, jnp.where(ri > ci, g, 0.0), HIGHEST)
        dec = jnp.exp(jnp.where(ri >= ci, diff, -jnp.inf))
        kb = k.astype(BF16)
        kk = _dot_nt(kb, kb)
        qk = _dot_nt(q.astype(BF16), kb)
        nmat = jnp.where(ri > ci, beta * kk * dec, 0.0)
        p = -nmat
        tinv = eye + p
        for _ in range(n_double):
            p = _dot(p, p, HIGHEST)
            tinv = tinv + _dot(tinv, p, HIGHEST)
        egc = jnp.exp(gc)
        u0 = _dot(tinv, beta * v, HIGHEST)
        w = _dot(tinv, (beta * egc) * k, HIGHEST)
        s = s_ref[0, h]
        sb = s.astype(BF16)
        u = u0 - _dot(w.astype(BF16), sb)
        attn = (qk * dec).astype(BF16)
        ub = u.astype(BF16)
        o = _dot((q * egc).astype(BF16), sb) + _dot(attn, ub)
        k_tail = (k * jnp.exp(g_last - gc)).astype(BF16)
        s_ref[0, h] = jnp.exp(g_last) * s + _dot_tn(k_tail, ub)
        on = o * lax.rsqrt(jnp.mean(o * o, axis=-1, keepdims=True) + EPS) * ng_ref[...]
        o_ref[:, hs] = (on * _silu(z_ref[:, hs])).astype(BF16)


def _gdn_chunks(q, k, v, gb, z, z_col, s0, norm_g, nseq, nchunk, c):
    row = lambda b, n: (b * nchunk + n, 0)
    st = pl.BlockSpec((1, GDN_HEADS, GDN_DK, GDN_DV), lambda b, n: (b, 0, 0, 0))
    return pl.pallas_call(
        _gdn_chunk_kernel,
        grid=(nseq, nchunk),
        in_specs=[pl.BlockSpec((c, GDN_QK), row), pl.BlockSpec((c, GDN_QK), row), pl.BlockSpec((c, GDN_QK), row),
                  pl.BlockSpec((c, LANES), row),
                  pl.BlockSpec((c, GDN_QK), lambda b, n: (b * nchunk + n, z_col)),
                  st, pl.BlockSpec((1, GDN_DV), lambda b, n: (0, 0))],
        out_specs=[pl.BlockSpec((c, GDN_QK), row), st],
        out_shape=[jax.ShapeDtypeStruct((nseq * nchunk * c, GDN_QK), BF16),
                   jax.ShapeDtypeStruct((nseq, GDN_HEADS, GDN_DK, GDN_DV), F32)],
        compiler_params=_cparams(("parallel", "arbitrary")),
        name="gdn_chunks_c%d" % c,
    )(q, k, v, gb, z, s0, norm_g.reshape(1, -1))


def _merge_kernel(c_ref, m_ref, g_ref, wc_ref, wm_ref, wg_ref, g0_ref, g1_ref, g2_ref, o_ref):
    y = (_sigmoid(g0_ref[...]) * _dot(c_ref[...], wc_ref[...])
         + _sigmoid(g1_ref[...]) * _dot(m_ref[...], wm_ref[...])
         + _sigmoid(g2_ref[...]) * _dot(g_ref[...], wg_ref[...]))
    o_ref[...] = y.astype(BF16)


def _merge(c_act, o_mla, o_gdn, w_pw, w_mo, w_go, act):
    n = c_act.shape[0]
    tm = _row_tile(n, 512)
    tn = 512
    nj = D_MODEL // tn
    a = pl.BlockSpec((tm, CONV_CH), lambda j, i: (i, 0))
    w = pl.BlockSpec((CONV_CH, tn), lambda j, i: (0, j))

    def gate(br):
        return pl.BlockSpec((tm, tn), lambda j, i: (i, br * nj + j))

    return pl.pallas_call(
        _merge_kernel,
        grid=(nj, n // tm),
        in_specs=[a, a, a, w, w, w, gate(0), gate(1), gate(2)],
        out_specs=pl.BlockSpec((tm, tn), lambda j, i: (i, j)),
        out_shape=jax.ShapeDtypeStruct((n, D_MODEL), BF16),
        compiler_params=_cparams(("parallel", "parallel")),
        name="merge_branches",
    )(c_act, o_mla, o_gdn, w_pw, w_mo, w_go, act, act, act)


def _out_ln_kernel(alpha, m_ref, w_ref, x_ref, g_ref, b_ref, o_ref):
    y = alpha * x_ref[...] + _dot(m_ref[...], w_ref[...])
    o_ref[...] = _ln_rows(y, g_ref[...], b_ref[...])


def _out_ln(merged, w_out, xt, ln_g, ln_b, alpha):
    n = xt.shape[0]
    tm = ROW_TILE
    vec = pl.BlockSpec((1, D_MODEL), lambda i: (0, 0))
    return pl.pallas_call(
        functools.partial(_out_ln_kernel, alpha),
        grid=(n // tm,),
        in_specs=[pl.BlockSpec((tm, D_MODEL), lambda i: (i, 0)),
                  pl.BlockSpec((D_MODEL, D_MODEL), lambda i: (0, 0)),
                  pl.BlockSpec((tm, D_MODEL), lambda i: (i, 0)), vec, vec],
        out_specs=pl.BlockSpec((tm, D_MODEL), lambda i: (i, 0)),
        out_shape=jax.ShapeDtypeStruct((n, D_MODEL), F32),
        compiler_params=_cparams(("parallel",)),
        name="out_proj_ln1",
    )(merged, w_out, xt, ln_g.reshape(1, -1), ln_b.reshape(1, -1))


def _router_kernel(x_ref, w_ref, b_ref, e_ref, p_ref):
    logits = _dot(x_ref[...], w_ref[...], HIGHEST) + b_ref[...]
    lane = lax.broadcasted_iota(jnp.int32, logits.shape, 1).astype(F32)
    big = jnp.float32(1e9)
    lg = jnp.where(lane < N_GROUPS, logits, -jnp.inf)
    mg = jnp.max(lg, axis=-1, keepdims=True)
    g_val = 1.0 / jnp.sum(jnp.exp(lg - mg), axis=-1, keepdims=True)
    g_idx = jnp.min(jnp.where(lg == mg, lane, big), axis=-1, keepdims=True)
    lo = N_GROUPS + g_idx * EXPERTS_PER_GROUP
    le = jnp.where((lane >= lo) & (lane < lo + EXPERTS_PER_GROUP), logits, -jnp.inf)
    m1 = jnp.max(le, axis=-1, keepdims=True)
    i1 = jnp.min(jnp.where(le == m1, lane, big), axis=-1, keepdims=True)
    le2 = jnp.where(lane == i1, -jnp.inf, le)
    m2 = jnp.max(le2, axis=-1, keepdims=True)
    i2 = jnp.min(jnp.where(le2 == m2, lane, big), axis=-1, keepdims=True)
    se = jnp.sum(jnp.exp(le - m1), axis=-1, keepdims=True)
    p1 = 1.0 / se
    p2 = jnp.exp(m2 - m1) / se
    w1 = g_val * (p1 / (p1 + p2))
    w2 = g_val * (p2 / (p1 + p2))
    e_ref[...] = jnp.where(lane == 0, i1 - N_GROUPS, jnp.where(lane == 1, i2 - N_GROUPS, 0.0)).astype(jnp.int32)
    p_ref[...] = jnp.where(lane == 0, w1, jnp.where(lane == 1, w2, 0.0))


def _router(x1, w_r, b_r):
    n = x1.shape[0]
    tm = ROW_TILE
    return pl.pallas_call(
        _router_kernel,
        grid=(n // tm,),
        in_specs=[pl.BlockSpec((tm, D_MODEL), lambda i: (i, 0)),
                  pl.BlockSpec((D_MODEL, LANES), lambda i: (0, 0)),
                  pl.BlockSpec((1, LANES), lambda i: (0, 0))],
        out_specs=[pl.BlockSpec((tm, LANES), lambda i: (i, 0)), pl.BlockSpec((tm, LANES), lambda i: (i, 0))],
        out_shape=[jax.ShapeDtypeStruct((n, LANES), jnp.int32), jax.ShapeDtypeStruct((n, LANES), F32)],
        compiler_params=_cparams(("parallel",)),
        name="moe_router",
    )(x1, w_r, b_r)


def _gather_rows(src_hbm, dst, sem, idx_ref, base, count):
    def body(r, carry):
        pltpu.make_async_copy(src_hbm.at[pl.ds(idx_ref[base + r], 1)], dst.at[pl.ds(r, 1)], sem).start()
        return carry
    lax.fori_loop(0, count, body, 0)


def _wait_rows(src_hbm, dst, sem, count):
    pltpu.make_async_copy(src_hbm.at[pl.ds(0, count)], dst.at[pl.ds(0, count)], sem).wait()


def _moe_kernel(layer, te_ref, src_ref, nu_ref, x_hbm, rw_ref, wg_ref, wu_ref, wd_ref, y_ref,
                xbuf, sem, wgb, wub, wdb):
    t = pl.program_id(0)
    nt = pl.num_programs(0)
    n_used = nu_ref[0]
    slot = t % 2
    te = MOE_TILE

    @pl.when(t == 0)
    def _():
        _gather_rows(x_hbm, xbuf.at[0], sem.at[0], src_ref, 0, te)

    @pl.when(t + 1 < n_used)
    def _():
        _gather_rows(x_hbm, xbuf.at[1 - slot], sem.at[1 - slot], src_ref, (t + 1) * te, te)

    changed = jnp.logical_or(t == 0, te_ref[t] != te_ref[jnp.maximum(t - 1, 0)])

    @pl.when(jnp.logical_and(changed, t < n_used))
    def _():
        wgb[...] = wg_ref[...].astype(BF16)
        wub[...] = wu_ref[...].astype(BF16)
        wdb[...] = wd_ref[...].astype(BF16)

    @pl.when(jnp.logical_or(t < n_used, t == 0))
    def _():
        _wait_rows(x_hbm, xbuf.at[slot], sem.at[slot], te)

    @pl.when(t < n_used)
    def _():
        xb = xbuf[slot].astype(BF16)
        h = _silu(_dot(xb, wgb[...])) * _dot(xb, wub[...]) * rw_ref[...]
        y_ref[...] = _dot(h.astype(BF16), wdb[...])

    @pl.when(t >= n_used)
    def _():
        y_ref[...] = jnp.zeros_like(y_ref)


def _moe_experts(layer, tile_expert, row_src, n_used, x1, row_w, w_gate, w_up, w_down):
    n_tiles = tile_expert.shape[0]
    te = MOE_TILE
    grid_spec = pltpu.PrefetchScalarGridSpec(
        num_scalar_prefetch=3,
        grid=(n_tiles,),
        in_specs=[pl.BlockSpec(memory_space=pl.ANY),
                  pl.BlockSpec((te, 1), lambda t, te_r, s_r, n_r: (t, 0)),
                  pl.BlockSpec((None, None, D_MODEL, EXPERT_FF), lambda t, te_r, s_r, n_r: (layer, te_r[t], 0, 0)),
                  pl.BlockSpec((None, None, D_MODEL, EXPERT_FF), lambda t, te_r, s_r, n_r: (layer, te_r[t], 0, 0)),
                  pl.BlockSpec((None, None, EXPERT_FF, D_MODEL), lambda t, te_r, s_r, n_r: (layer, te_r[t], 0, 0))],
        out_specs=pl.BlockSpec((te, D_MODEL), lambda t, te_r, s_r, n_r: (t, 0)),
        scratch_shapes=[pltpu.VMEM((2, te, D_MODEL), F32), pltpu.SemaphoreType.DMA((2,)),
                        pltpu.VMEM((D_MODEL, EXPERT_FF), BF16), pltpu.VMEM((D_MODEL, EXPERT_FF), BF16),
                        pltpu.VMEM((EXPERT_FF, D_MODEL), BF16)],
    )
    return pl.pallas_call(
        functools.partial(_moe_kernel, layer),
        grid_spec=grid_spec,
        out_shape=jax.ShapeDtypeStruct((n_tiles * te, D_MODEL), F32),
        compiler_params=_cparams(("arbitrary",)),
        name="moe_experts",
    )(tile_expert, row_src, n_used, x1, row_w, w_gate, w_up, w_down)


def _combine_ln_kernel(alpha, pos_ref, y_hbm, x_ref, g_ref, b_ref, o_ref, ybuf, sem):
    t = pl.program_id(0)
    nt = pl.num_programs(0)
    tm = x_ref.shape[0]
    slot = t % 2

    @pl.when(t == 0)
    def _():
        _gather_rows(y_hbm, ybuf.at[0], sem.at[0], pos_ref, 0, 2 * tm)

    @pl.when(t + 1 < nt)
    def _():
        _gather_rows(y_hbm, ybuf.at[1 - slot], sem.at[1 - slot], pos_ref, (t + 1) * 2 * tm, 2 * tm)

    _wait_rows(y_hbm, ybuf.at[slot], sem.at[slot], 2 * tm)
    y = alpha * x_ref[...] + ybuf[slot, 0:tm, :] + ybuf[slot, tm:2 * tm, :]
    o_ref[...] = _ln_rows(y, g_ref[...], b_ref[...])


def _combine_ln(pos, y_sorted, x1, ln_g, ln_b, alpha):
    n = x1.shape[0]
    tm = ROW_TILE
    vec = pl.BlockSpec((1, D_MODEL), lambda i, p: (0, 0))
    grid_spec = pltpu.PrefetchScalarGridSpec(
        num_scalar_prefetch=1,
        grid=(n // tm,),
        in_specs=[pl.BlockSpec(memory_space=pl.ANY),
                  pl.BlockSpec((tm, D_MODEL), lambda i, p: (i, 0)), vec, vec],
        out_specs=pl.BlockSpec((tm, D_MODEL), lambda i, p: (i, 0)),
        scratch_shapes=[pltpu.VMEM((2, 2 * tm, D_MODEL), F32), pltpu.SemaphoreType.DMA((2,))],
    )
    return pl.pallas_call(
        functools.partial(_combine_ln_kernel, alpha),
        grid_spec=grid_spec,
        out_shape=jax.ShapeDtypeStruct((n, D_MODEL), F32),
        compiler_params=_cparams(("arbitrary",)),
        name="moe_combine_ln2",
    )(pos, y_sorted, x1, ln_g.reshape(1, -1), ln_b.reshape(1, -1))


def _routing_tables(eid, wts, n):
    te = MOE_TILE
    n_tiles = (2 * n) // te + N_EXPERTS
    e = eid.reshape(-1)
    onehot = (e[:, None] == jnp.arange(N_EXPERTS, dtype=jnp.int32)[None, :]).astype(jnp.int32)
    rank = jnp.sum((jnp.cumsum(onehot, axis=0) - onehot) * onehot, axis=1)
    counts = jnp.sum(onehot, axis=0)
    tiles_per = (counts + te - 1) // te
    tile_end = jnp.cumsum(tiles_per)
    tile_start = tile_end - tiles_per
    dest = tile_start[e] * te + rank
    n_used = tile_end[-1:].astype(jnp.int32)
    tile_expert = jnp.minimum(jnp.searchsorted(tile_end, jnp.arange(n_tiles, dtype=jnp.int32), side="right"),
                              N_EXPERTS - 1).astype(jnp.int32)
    token = jnp.arange(2 * n, dtype=jnp.int32) // 2
    row_src = jnp.zeros((n_tiles * te,), jnp.int32).at[dest].set(token)
    row_w = jnp.zeros((n_tiles * te,), F32).at[dest].set(wts.reshape(-1))
    tm = ROW_TILE
    pos = dest.reshape(n // tm, tm, 2).transpose(0, 2, 1).reshape(-1).astype(jnp.int32)
    return tile_expert, row_src, row_w.reshape(-1, 1), n_used, pos


def _rot_cols(w):
    half = ROPE_DIM // 2
    return jnp.concatenate([-w[..., half:], w[..., :half]], axis=-1)


def _pack_w_in(w_in):
    d = w_in.shape[0]
    offs = np.cumsum([0, 2 * CONV_CH, Q_LORA, KV_LORA, ROPE_DIM, GDN_QKV, GDN_HEADS * GDN_DV, GDN_HEADS, GDN_HEADS,
                      N_BRANCHES * D_MODEL])
    seg = [w_in[:, offs[i]:offs[i + 1]] for i in range(9)]
    glu, cq, ckv, kr, qkv, z, b_raw, a_raw, gate = seg
    ba = jnp.concatenate([b_raw, a_raw, jnp.zeros((d, LANES - 2 * GDN_HEADS), w_in.dtype)], axis=1)
    cols = [gate, qkv, glu[:, :CONV_CH], glu[:, CONV_CH:], z, cq, ckv, kr, _rot_cols(kr), ba,
            jnp.zeros((d, N_PACK - COL_BA - LANES), w_in.dtype)]
    return jnp.concatenate(cols, axis=1)


def _pack_w_uq(w_uq):
    w = w_uq.reshape(Q_LORA, MLA_HEADS, NOPE_DIM + ROPE_DIM)
    nope, rope = w[..., :NOPE_DIM], w[..., NOPE_DIM:]
    zpad = jnp.zeros((Q_LORA, MLA_HEADS, HEAD_QK - NOPE_DIM - ROPE_DIM), w.dtype)
    wa = jnp.concatenate([nope, rope, zpad], axis=-1)
    wb = jnp.concatenate([jnp.zeros_like(nope), _rot_cols(rope), zpad], axis=-1)
    return wa.reshape(Q_LORA, -1).astype(BF16), wb.reshape(Q_LORA, -1).astype(BF16)


def _rope_tables(pos, scale):
    half = ROPE_DIM // 2
    inv_freq = ROPE_THETA ** (-jnp.arange(half, dtype=F32) / half)
    ang = pos.astype(F32)[:, None] * inv_freq[None, :]
    cos = jnp.concatenate([jnp.cos(ang), jnp.cos(ang)], axis=-1)
    sin = jnp.concatenate([jnp.sin(ang), jnp.sin(ang)], axis=-1)
    n = pos.shape[0]
    zpad = jnp.zeros((n, HEAD_QK - NOPE_DIM - ROPE_DIM), F32)
    tab_a = jnp.concatenate([jnp.full((n, NOPE_DIM), scale, F32), scale * cos, zpad], axis=-1)
    tab_b = jnp.concatenate([jnp.zeros((n, NOPE_DIM), F32), scale * sin, zpad], axis=-1)
    tab_cs = jnp.concatenate([cos, sin], axis=-1)
    return tab_a, tab_b, tab_cs


def kernel(x_prompt, x_sample, cache_ckv, cache_krope, state_conf_conv, state_gdn_conv, state_gdn, page_table, w_in, conv_dw, conv_dw_b, conv_ln_g, conv_ln_b, conv_pw, mla_q_norm, mla_w_uq, mla_kv_norm, mla_w_uk, mla_w_uv, mla_w_o, gdn_conv_w, gdn_a_log, gdn_dt_bias, gdn_norm, gdn_w_o, w_out, ln1_g, ln1_b, router_group_w, router_group_b, router_expert_w, router_expert_b, moe_w_gate, moe_w_up, moe_w_down, ln2_g, ln2_b):
    bp, tp, d = x_prompt.shape
    bs, ts, _ = x_sample.shape
    depth = w_in.shape[0]
    n_pages, page = page_table.shape[1], cache_ckv.shape[2]
    past_len = n_pages * page
    n_p, n_s = bp * tp, bs * ts
    n = n_p + n_s
    assert d == D_MODEL and w_in.shape[2] == 2 * CONV_CH + Q_LORA + KV_LORA + ROPE_DIM + GDN_QKV + GDN_HEADS * GDN_DV \
        + 2 * GDN_HEADS + N_BRANCHES * D_MODEL
    assert tp % ATT_TILE == 0 and tp % GDN_CHUNK == 0 and n % ROW_TILE == 0 and n_p % ROW_TILE == 0
    assert ts <= NEW_PAD and bs % SAMPLE_BB == 0 and n_p % (SAMPLE_BB * ts) == 0 and n_pages % PAGES_PER_STEP == 0
    assert (2 * n) % MOE_TILE == 0
    alpha = float((2.0 * depth) ** 0.25)
    scale = float((NOPE_DIM + ROPE_DIM) ** -0.5)

    pos = jnp.concatenate([jnp.tile(jnp.arange(tp, dtype=jnp.int32), bp),
                           jnp.tile(past_len + jnp.arange(ts, dtype=jnp.int32), bs)])
    tab_a, tab_b, tab_cs = _rope_tables(pos, scale)
    e_place = jnp.zeros((LANES, MLA_HEADS, HEAD_QK), F32)
    e_place = e_place.at[jnp.arange(ROPE_DIM)[:, None], jnp.arange(MLA_HEADS)[None, :],
                         NOPE_DIM + jnp.arange(ROPE_DIM)[:, None]].set(1.0).reshape(LANES, -1).astype(BF16)
    zeros_state = jnp.zeros((bp, GDN_HEADS, GDN_DK, GDN_DV), F32)

    xt = jnp.concatenate([x_prompt.reshape(n_p, d), x_sample.reshape(n_s, d)], axis=0)
    outs = [[] for _ in range(10)]
    for l in range(depth):
        w_pack = _pack_w_in(w_in[l])
        w_qa, w_qb = _pack_w_uq(mla_w_uq[l])
        w_uk = mla_w_uk[l]
        w_uk_pad = jnp.concatenate([w_uk, jnp.zeros((KV_LORA, MLA_HEADS, HEAD_QK - NOPE_DIM), F32)],
                                   axis=-1).reshape(KV_LORA, -1).astype(BF16)
        w_uk_t = jnp.transpose(w_uk, (1, 2, 0)).astype(BF16)
        w_uv = mla_w_uv[l].reshape(KV_LORA, -1).astype(BF16)
        al = jnp.zeros((2, LANES), F32).at[0, GDN_HEADS:2 * GDN_HEADS].set(gdn_a_log[l]) \
            .at[1, GDN_HEADS:2 * GDN_HEADS].set(gdn_dt_bias[l])
        w_r = jnp.concatenate([router_group_w[l], router_expert_w[l],
                               jnp.zeros((d, LANES - N_GROUPS - N_EXPERTS), F32)], axis=1)
        b_r = jnp.concatenate([router_group_b[l], router_expert_b[l],
                               jnp.zeros((LANES - N_GROUPS - N_EXPERTS,), F32)]).reshape(1, LANES)

        act = _inproj(xt, w_pack)

        c_p, u_p = _conf_prompt(act, bp, tp, conv_dw[l], conv_dw_b[l], conv_ln_g[l], conv_ln_b[l])
        c_s, conf_buf_s = _conf_sample(act, n_p, bs, ts, state_conf_conv[l], conv_dw[l], conv_dw_b[l],
                                       conv_ln_g[l], conv_ln_b[l])
        c_act = jnp.concatenate([c_p, c_s], axis=0)
        conf_buf_p = u_p.reshape(bp, tp, CONV_CH)[:, tp - (CONV_WIDTH - 1):]

        q = _mla_q(act, mla_q_norm[l], w_qa, w_qb, tab_a, tab_b)
        ckv, krope, k_full, v_full = _mla_kv(act, mla_kv_norm[l], tab_cs, w_uk_pad, w_uv, e_place)
        o_p = _flash_prompt(q, k_full, v_full, bp, tp)
        q_abs = _q_absorb(q[n_p:], w_uk_t).reshape(bs, ts * MLA_HEADS, Q_ABS)
        ckv_s = ckv[n_p:].reshape(bs, ts, KV_LORA)
        kr_s = krope[n_p:].reshape(bs, ts, ROPE_DIM)
        pad = ((0, 0), (0, NEW_PAD - ts), (0, 0))
        o_lat = _paged_attention(l, page_table, q_abs, jnp.pad(ckv_s, pad), jnp.pad(kr_s, pad),
                                 cache_ckv, cache_krope, ts)
        o_s = _value_up(o_lat.reshape(n_s, MLA_HEADS * KV_LORA), w_uv)
        o_mla = jnp.concatenate([o_p, o_s], axis=0)

        qg_p, kg_p, vg_p, gb_p = _gdn_prep_prompt(act, bp, tp, gdn_conv_w[l], al)
        og_p, st_p = _gdn_chunks(qg_p, kg_p, vg_p, gb_p, act, COL_Z // GDN_QK, zeros_state, gdn_norm[l],
                                 bp, tp // GDN_CHUNK, GDN_CHUNK)
        qg_s, kg_s, vg_s, gb_s = _gdn_prep_sample(act, n_p, bs, ts, state_gdn_conv[l], gdn_conv_w[l], al)
        z_s = jnp.pad(act[n_p:, COL_Z:COL_Z + GDN_QK].reshape(bs, ts, GDN_QK), pad).reshape(bs * NEW_PAD, GDN_QK)
        og_s, st_s = _gdn_chunks(qg_s, kg_s, vg_s, gb_s, z_s, 0, state_gdn[l], gdn_norm[l], bs, 1, NEW_PAD)
        og_s = og_s.reshape(bs, NEW_PAD, GDN_QK)[:, :ts].reshape(n_s, GDN_QK)
        o_gdn = jnp.concatenate([og_p, og_s], axis=0)
        qkv_raw = act[:, COL_QKV:COL_QKV + GDN_QKV]
        gconv_p = qkv_raw[:n_p].reshape(bp, tp, GDN_QKV)[:, tp - (SHORT_CONV - 1):]
        gconv_s = jnp.concatenate([state_gdn_conv[l], qkv_raw[n_p:].reshape(bs, ts, GDN_QKV)],
                                  axis=1)[:, -(SHORT_CONV - 1):]

        merged = _merge(c_act, o_mla, o_gdn, conv_pw[l].astype(BF16), mla_w_o[l].astype(BF16),
                        gdn_w_o[l].astype(BF16), act)
        x1 = _out_ln(merged, w_out[l].astype(BF16), xt, ln1_g[l], ln1_b[l], alpha)

        eid, wts = _router(x1, w_r, b_r)
        tile_expert, row_src, row_w, n_used, pos_tab = _routing_tables(eid[:, :2], wts[:, :2], n)
        y_sorted = _moe_experts(l, tile_expert, row_src, n_used, x1, row_w, moe_w_gate, moe_w_up, moe_w_down)
        xt = _combine_ln(pos_tab, y_sorted, x1, ln2_g[l], ln2_b[l], alpha)

        new = [ckv[:n_p].reshape(bp, tp, KV_LORA), krope[:n_p].reshape(bp, tp, ROPE_DIM), ckv_s, kr_s,
               conf_buf_p, conf_buf_s, gconv_p, gconv_s, st_p, st_s]
        for lst, val in zip(outs, new):
            lst.append(val)

    return (xt[:n_p].reshape(bp, tp, d), xt[n_p:].reshape(bs, ts, d)) + tuple(jnp.stack(o) for o in outs)
```

```python
import functools

import numpy as np
import jax
import jax.numpy as jnp
from jax import lax
from jax.experimental import pallas as pl
from jax.experimental.pallas import tpu as pltpu

F32 = jnp.float32
BF16 = jnp.bfloat16
HIGHEST = lax.Precision.HIGHEST

D_MODEL = 2048
CONV_CH = 1024
CONV_WIDTH = 31
MLA_HEADS = 8
Q_LORA = 512
KV_LORA = 512
NOPE_DIM = 128
ROPE_DIM = 64
V_DIM = 128
ROPE_THETA = 10000.0
GDN_HEADS = 8
GDN_DK = 128
GDN_DV = 128
GDN_QK = GDN_HEADS * GDN_DK
GDN_QKV = 2 * GDN_QK + GDN_HEADS * GDN_DV
SHORT_CONV = 4
GDN_CHUNK = 64
N_GROUPS = 4
EXPERTS_PER_GROUP = 8
N_EXPERTS = N_GROUPS * EXPERTS_PER_GROUP
EXPERT_FF = 512
N_BRANCHES = 3
EPS = 1e-6

LANES = 128
SUBLANES = 8
VMEM_LIMIT_BYTES = 56 * 1024 * 1024

COL_GATE = 0
COL_QKV = 6144
COL_A = 9216
COL_G = 10240
COL_Z = 11264
COL_CQ = 12288
COL_CKV = 12800
COL_KR = 13312
COL_BA = 13440
N_PACK = 13824
TN_IN = 512

HEAD_QK = 256
Q_ABS = 640
MOE_TILE = 256
ROW_TILE = 256


def _cparams(sem):
    return pltpu.CompilerParams(dimension_semantics=sem, vmem_limit_bytes=VMEM_LIMIT_BYTES)


def _dot(a, b, precision=None):
    return jnp.dot(a, b, preferred_element_type=F32, precision=precision)


def _dot_nt(a, b, precision=None):
    return lax.dot_general(a, b, (((1,), (1,)), ((), ())), preferred_element_type=F32, precision=precision)


def _dot_tn(a, b, precision=None):
    return lax.dot_general(a, b, (((0,), (0,)), ((), ())), preferred_element_type=F32, precision=precision)


def _sigmoid(x):
    return 1.0 / (1.0 + jnp.exp(-x))


def _silu(x):
    return x * _sigmoid(x)


def _row_tile(n, cap):
    best = SUBLANES
    for t in range(SUBLANES, cap + 1, SUBLANES):
        if n % t == 0:
            best = t
    return best


def _inproj_kernel(x_ref, w_ref, o_ref, xb_ref):
    @pl.when(pl.program_id(1) == 0)
    def _():
        xb_ref[...] = x_ref[...].astype(BF16)

    o_ref[...] = _dot_nt(xb_ref[...], w_ref[...].astype(BF16))


def _inproj(xt, w_pack_t):
    n, d = xt.shape
    tm = _row_tile(n, 1100)
    return pl.pallas_call(
        _inproj_kernel,
        grid=(n // tm, N_PACK // TN_IN),
        in_specs=[pl.BlockSpec((tm, d), lambda i, j: (i, 0)),
                  pl.BlockSpec((TN_IN, d), lambda i, j: (j, 0))],
        out_specs=pl.BlockSpec((tm, TN_IN), lambda i, j: (i, j)),
        out_shape=jax.ShapeDtypeStruct((n, N_PACK), F32),
        scratch_shapes=[pltpu.VMEM((tm, d), BF16)],
        compiler_params=_cparams(("parallel", "arbitrary")),
        name="inproj",
    )(xt, w_pack_t)


def _ln_rows(y, g, b):
    mu = jnp.mean(y, axis=-1, keepdims=True)
    yc = y - mu
    var = jnp.mean(yc * yc, axis=-1, keepdims=True)
    return yc * lax.rsqrt(var + EPS) * g + b


CONV_HALO = 32


def _conf_prompt_kernel(a_ref, g_ref, ap_ref, gp_ref, w_ref, b_ref, lg_ref, lb_ref, act_ref, u_ref, xp_ref, y_ref):
    tt = a_ref.shape[0]
    u = a_ref[...] * _sigmoid(g_ref[...])
    u_ref[...] = u
    xp_ref[CONV_HALO:, :] = u
    up = ap_ref[...] * _sigmoid(gp_ref[...])
    xp_ref[0:CONV_HALO, :] = jnp.where(pl.program_id(1) > 0, up, 0.0)
    off = CONV_HALO - (CONV_WIDTH - 1)
    for c in range(CONV_CH // LANES):
        cs = slice(c * LANES, (c + 1) * LANES)
        acc = jnp.zeros((tt, LANES), F32)
        for j in range(CONV_WIDTH):
            acc = acc + xp_ref[off + j:off + j + tt, cs] * w_ref[j:j + 1, cs]
        y_ref[:, cs] = acc + b_ref[:, cs]
    yn = _ln_rows(y_ref[...], lg_ref[...], lb_ref[...])
    act_ref[...] = _silu(yn).astype(BF16)


def _conf_prompt(act, bp, tp, conv_dw, dw_b, ln_g, ln_b):
    tt = ROW_TILE
    nt = tp // tt
    ca, cg = COL_A // CONV_CH, COL_G // CONV_CH
    r = tt // CONV_HALO

    def cur(col):
        return pl.BlockSpec((tt, CONV_CH), lambda b, i: (b * nt + i, col))

    def prev(col):
        return pl.BlockSpec((CONV_HALO, CONV_CH), lambda b, i: (jnp.maximum((b * nt + i) * r - 1, 0), col))

    vec = pl.BlockSpec((1, CONV_CH), lambda b, i: (0, 0))
    return pl.pallas_call(
        _conf_prompt_kernel,
        grid=(bp, nt),
        in_specs=[cur(ca), cur(cg), prev(ca), prev(cg),
                  pl.BlockSpec((CONV_WIDTH, CONV_CH), lambda b, i: (0, 0)), vec, vec, vec],
        out_specs=[pl.BlockSpec((tt, CONV_CH), lambda b, i: (b * nt + i, 0)),
                   pl.BlockSpec((tt, CONV_CH), lambda b, i: (b * nt + i, 0))],
        out_shape=[jax.ShapeDtypeStruct((bp * tp, CONV_CH), BF16),
                   jax.ShapeDtypeStruct((bp * tp, CONV_CH), F32)],
        scratch_shapes=[pltpu.VMEM((tt + CONV_HALO, CONV_CH), F32), pltpu.VMEM((tt, CONV_CH), F32)],
        compiler_params=_cparams(("parallel", "arbitrary")),
        name="conf_conv_prompt",
    )(act, act, act, act, conv_dw, dw_b.reshape(1, -1), ln_g.reshape(1, -1), ln_b.reshape(1, -1))


SAMPLE_BB = 8


def _conf_sample_kernel(ts, a_ref, g_ref, buf_ref, w_ref, b_ref, lg_ref, lb_ref, act_ref, nbuf_ref, xp_ref, y_ref):
    nb = CONV_WIDTH - 1
    u = a_ref[...] * _sigmoid(g_ref[...])
    w = w_ref[...]
    for bi in range(SAMPLE_BB):
        xp_ref[0:nb, :] = buf_ref[bi]
        xp_ref[nb:nb + ts, :] = u[bi * ts:(bi + 1) * ts, :]
        nbuf_ref[bi] = xp_ref[ts:ts + nb, :]
        for t in range(ts):
            y_ref[bi * ts + t:bi * ts + t + 1, :] = jnp.sum(xp_ref[t:t + CONV_WIDTH, :] * w, axis=0, keepdims=True)
    yn = _ln_rows(y_ref[...] + b_ref[...], lg_ref[...], lb_ref[...])
    act_ref[...] = _silu(yn).astype(BF16)


def _conf_sample(act, n_p, bs, ts, buf, conv_dw, dw_b, ln_g, ln_b):
    bb = SAMPLE_BB
    rows = bb * ts
    nb = CONV_WIDTH - 1
    base = n_p // rows
    vec = pl.BlockSpec((1, CONV_CH), lambda i: (0, 0))
    return pl.pallas_call(
        functools.partial(_conf_sample_kernel, ts),
        grid=(bs // bb,),
        in_specs=[pl.BlockSpec((rows, CONV_CH), lambda i: (base + i, COL_A // CONV_CH)),
                  pl.BlockSpec((rows, CONV_CH), lambda i: (base + i, COL_G // CONV_CH)),
                  pl.BlockSpec((bb, nb, CONV_CH), lambda i: (i, 0, 0)),
                  pl.BlockSpec((CONV_WIDTH, CONV_CH), lambda i: (0, 0)), vec, vec, vec],
        out_specs=[pl.BlockSpec((rows, CONV_CH), lambda i: (i, 0)),
                   pl.BlockSpec((bb, nb, CONV_CH), lambda i: (i, 0, 0))],
        out_shape=[jax.ShapeDtypeStruct((bs * ts, CONV_CH), BF16),
                   jax.ShapeDtypeStruct((bs, nb, CONV_CH), F32)],
        scratch_shapes=[pltpu.VMEM((nb + ts + 6, CONV_CH), F32), pltpu.VMEM((rows, CONV_CH), F32)],
        compiler_params=_cparams(("parallel",)),
        name="conf_conv_sample",
    )(act, act, buf, conv_dw, dw_b.reshape(1, -1), ln_g.reshape(1, -1), ln_b.reshape(1, -1))


def _rms_rows(x, g):
    return x * lax.rsqrt(jnp.mean(x * x, axis=-1, keepdims=True) + EPS) * g


def _mla_q_kernel(cq_ref, g_ref, wa_ref, wb_ref, ca_ref, cb_ref, q_ref):
    cqn = _rms_rows(cq_ref[...], g_ref[...]).astype(BF16)
    qa = _dot(cqn, wa_ref[...])
    qb = _dot(cqn, wb_ref[...])
    ca = ca_ref[...]
    cb = cb_ref[...]
    for h in range(MLA_HEADS):
        hs = slice(h * HEAD_QK, (h + 1) * HEAD_QK)
        q_ref[:, hs] = (qa[:, hs] * ca + qb[:, hs] * cb).astype(BF16)


def _mla_q(act, q_norm, w_qa, w_qb, tab_a, tab_b):
    n = act.shape[0]
    tm = ROW_TILE
    nq = MLA_HEADS * HEAD_QK
    return pl.pallas_call(
        _mla_q_kernel,
        grid=(n // tm,),
        in_specs=[pl.BlockSpec((tm, Q_LORA), lambda i: (i, COL_CQ // Q_LORA)),
                  pl.BlockSpec((1, Q_LORA), lambda i: (0, 0)),
                  pl.BlockSpec((Q_LORA, nq), lambda i: (0, 0)),
                  pl.BlockSpec((Q_LORA, nq), lambda i: (0, 0)),
                  pl.BlockSpec((tm, HEAD_QK), lambda i: (i, 0)),
                  pl.BlockSpec((tm, HEAD_QK), lambda i: (i, 0))],
        out_specs=pl.BlockSpec((tm, nq), lambda i: (i, 0)),
        out_shape=jax.ShapeDtypeStruct((n, nq), BF16),
        compiler_params=_cparams(("parallel",)),
        name="mla_q",
    )(act, q_norm.reshape(1, -1), w_qa, w_qb, tab_a, tab_b)


def _mla_kv_kernel(ckv_ref, krr_ref, g_ref, cs_ref, wuk_ref, wuv_ref, e_ref, ckv_out, kr_out, k_out, v_out):
    ckvn = _rms_rows(ckv_ref[...], g_ref[...])
    ckv_out[...] = ckvn
    t = krr_ref[...] * cs_ref[...]
    kr2 = t + pltpu.roll(t, ROPE_DIM, axis=1)
    kr_out[...] = kr2[:, :ROPE_DIM]
    cb = ckvn.astype(BF16)
    k_out[...] = (_dot(cb, wuk_ref[...]) + _dot(kr2.astype(BF16), e_ref[...])).astype(BF16)
    v_out[...] = _dot(cb, wuv_ref[...]).astype(BF16)


def _mla_kv(act, kv_norm, tab_cs, w_uk_pad, w_uv, e_place):
    n = act.shape[0]
    tm = ROW_TILE
    nk = MLA_HEADS * HEAD_QK
    nv = MLA_HEADS * V_DIM
    return pl.pallas_call(
        _mla_kv_kernel,
        grid=(n // tm,),
        in_specs=[pl.BlockSpec((tm, KV_LORA), lambda i: (i, COL_CKV // KV_LORA)),
                  pl.BlockSpec((tm, LANES), lambda i: (i, COL_KR // LANES)),
                  pl.BlockSpec((1, KV_LORA), lambda i: (0, 0)),
                  pl.BlockSpec((tm, LANES), lambda i: (i, 0)),
                  pl.BlockSpec((KV_LORA, nk), lambda i: (0, 0)),
                  pl.BlockSpec((KV_LORA, nv), lambda i: (0, 0)),
                  pl.BlockSpec((LANES, nk), lambda i: (0, 0))],
        out_specs=[pl.BlockSpec((tm, KV_LORA), lambda i: (i, 0)),
                   pl.BlockSpec((tm, ROPE_DIM), lambda i: (i, 0)),
                   pl.BlockSpec((tm, nk), lambda i: (i, 0)),
                   pl.BlockSpec((tm, nv), lambda i: (i, 0))],
        out_shape=[jax.ShapeDtypeStruct((n, KV_LORA), F32),
                   jax.ShapeDtypeStruct((n, ROPE_DIM), F32),
                   jax.ShapeDtypeStruct((n, nk), BF16),
                   jax.ShapeDtypeStruct((n, nv), BF16)],
        compiler_params=_cparams(("parallel",)),
        name="mla_kv",
    )(act, act, kv_norm.reshape(1, -1), tab_cs, w_uk_pad, w_uv, e_place)


ATT_TILE = 512


ATT_SUB = 128


def _flash_kernel(qi_ref, ki_ref, q_ref, k_ref, v_ref, o_ref, m_ref, l_ref, acc_ref):
    pair = pl.program_id(2)
    qi = qi_ref[pair]
    ki = ki_ref[pair]
    t = q_ref.shape[0]
    rep = t // V_DIM

    @pl.when(ki == 0)
    def _():
        m_ref[...] = jnp.full_like(m_ref, -jnp.inf)
        l_ref[...] = jnp.zeros_like(l_ref)
        acc_ref[...] = jnp.zeros_like(acc_ref)

    def update(masked):
        k = k_ref[...]
        v = v_ref[...]
        rsl = [slice(sb * ATT_SUB, (sb + 1) * ATT_SUB) for sb in range(t // ATT_SUB)]
        scores = [_dot_nt(q_ref[rs, :], k) for rs in rsl]
        probs = []
        for sb, (rs, s) in enumerate(zip(rsl, scores)):
            if masked:
                rows = lax.broadcasted_iota(jnp.int32, s.shape, 0) + sb * ATT_SUB
                cols = lax.broadcasted_iota(jnp.int32, s.shape, 1)
                s = jnp.where(cols <= rows, s, -jnp.inf)
            m_old = m_ref[rs, :]
            m_new = jnp.maximum(m_old, jnp.broadcast_to(jnp.max(s, axis=-1, keepdims=True), m_old.shape))
            p = jnp.exp(s - jnp.concatenate([m_new] * rep, axis=1))
            alpha = jnp.exp(m_old - m_new)
            l_ref[rs, :] = alpha * l_ref[rs, :] + jnp.broadcast_to(jnp.sum(p, axis=-1, keepdims=True), m_old.shape)
            acc_ref[rs, :] = alpha * acc_ref[rs, :]
            m_ref[rs, :] = m_new
            probs.append(p.astype(BF16))
        for rs, p in zip(rsl, probs):
            acc_ref[rs, :] += _dot(p, v)

    @pl.when(ki < qi)
    def _():
        update(False)

    @pl.when(ki == qi)
    def _():
        update(True)
        o_ref[...] = (acc_ref[...] / l_ref[...]).astype(BF16)


def _flash_prompt(q, k, v, bp, tp):
    t = ATT_TILE
    nt = tp // t
    pairs = [(a, b) for a in range(nt) for b in range(a + 1)]
    qi_list = jnp.asarray([a for a, _ in pairs], jnp.int32)
    ki_list = jnp.asarray([b for _, b in pairs], jnp.int32)
    grid_spec = pltpu.PrefetchScalarGridSpec(
        num_scalar_prefetch=2,
        grid=(bp, MLA_HEADS, len(pairs)),
        in_specs=[pl.BlockSpec((t, HEAD_QK), lambda b, h, p, qi, ki: (b * nt + qi[p], h)),
                  pl.BlockSpec((t, HEAD_QK), lambda b, h, p, qi, ki: (b * nt + ki[p], h)),
                  pl.BlockSpec((t, V_DIM), lambda b, h, p, qi, ki: (b * nt + ki[p], h))],
        out_specs=pl.BlockSpec((t, V_DIM), lambda b, h, p, qi, ki: (b * nt + qi[p], h)),
        scratch_shapes=[pltpu.VMEM((t, V_DIM), F32), pltpu.VMEM((t, V_DIM), F32), pltpu.VMEM((t, V_DIM), F32)],
    )
    return pl.pallas_call(
        _flash_kernel,
        grid_spec=grid_spec,
        out_shape=jax.ShapeDtypeStruct((bp * tp, MLA_HEADS * V_DIM), BF16),
        compiler_params=_cparams(("parallel", "parallel", "arbitrary")),
        name="mla_flash_prompt",
    )(qi_list, ki_list, q, k, v)


def _q_absorb_kernel(q_ref, wt_ref, o_ref):
    q = q_ref[...]
    o_ref[:, :KV_LORA] = _dot(q[:, :NOPE_DIM], wt_ref[...]).astype(BF16)
    o_ref[:, KV_LORA:] = q[:, NOPE_DIM:]


def _q_absorb(q_s, w_uk_t):
    n_s = q_s.shape[0]
    return pl.pallas_call(
        _q_absorb_kernel,
        grid=(MLA_HEADS,),
        in_specs=[pl.BlockSpec((n_s, HEAD_QK), lambda h: (0, h)),
                  pl.BlockSpec((None, NOPE_DIM, KV_LORA), lambda h: (h, 0, 0))],
        out_specs=pl.BlockSpec((n_s, Q_ABS), lambda h: (0, h)),
        out_shape=jax.ShapeDtypeStruct((n_s, MLA_HEADS * Q_ABS), BF16),
        compiler_params=_cparams(("parallel",)),
        name="mla_q_absorb",
    )(q_s, w_uk_t)


PAGES_PER_STEP = 16
PAGES_PER_GROUP = 4
NEW_PAD = 8


def _paged_kernel(layer, ppb, page, ts, pt_ref, q_ref, cn_ref, kn_ref, ck_hbm, kr_hbm, o_ref,
                  ckbuf, krbuf, ck_sem, kr_sem, kc_s, kr_s, nc_s, m_ref, l_ref, acc_ref):
    b_idx = pl.program_id(0)
    s_idx = pl.program_id(1)
    n_steps = pl.num_programs(1)
    g = b_idx * n_steps + s_idx
    slot = g % 2
    gk = PAGES_PER_GROUP * page

    def fetch(gi, dst_slot):
        bb = gi // n_steps
        first = (gi % n_steps) * ppb
        for k in range(ppb):
            pid = pt_ref[bb, first + k]
            pltpu.make_async_copy(ck_hbm.at[layer, pid], ckbuf.at[dst_slot, k], ck_sem.at[dst_slot]).start()
            pltpu.make_async_copy(kr_hbm.at[layer, pid], krbuf.at[dst_slot, k], kr_sem.at[dst_slot]).start()

    @pl.when(g == 0)
    def _():
        fetch(g, 0)

    @pl.when(g + 1 < pl.num_programs(0) * n_steps)
    def _():
        fetch(g + 1, 1 - slot)

    pltpu.make_async_copy(ck_hbm.at[layer, pl.ds(0, ppb)], ckbuf.at[slot], ck_sem.at[slot]).wait()
    pltpu.make_async_copy(kr_hbm.at[layer, pl.ds(0, ppb)], krbuf.at[slot], kr_sem.at[slot]).wait()

    @pl.when(s_idx == 0)
    def _():
        m_ref[...] = jnp.full_like(m_ref, -jnp.inf)
        l_ref[...] = jnp.zeros_like(l_ref)
        acc_ref[...] = jnp.zeros_like(acc_ref)

    for k in range(ppb):
        kc_s[k * page:(k + 1) * page, :] = ckbuf[slot, k].astype(BF16)
        kr_s[:, k * page:(k + 1) * page] = krbuf[slot, k].astype(BF16)
    q = q_ref[0]
    q_lat = q[:, :KV_LORA]
    q_rope = q[:, KV_LORA:KV_LORA + ROPE_DIM]

    def update(scores, vals):
        m_old = m_ref[...]
        m_new = m_old
        for s in scores:
            m_new = jnp.maximum(m_new, jnp.max(s, axis=-1, keepdims=True))
        alpha = jnp.exp(m_old - m_new)
        l_new = alpha * l_ref[...]
        acc = alpha * acc_ref[...]
        for s, val in zip(scores, vals):
            p = jnp.exp(s - m_new)
            l_new = l_new + jnp.sum(p, axis=-1, keepdims=True)
            acc = acc + _dot(p.astype(BF16), val)
        l_ref[...] = l_new
        acc_ref[...] = acc
        m_ref[...] = m_new

    groups = [slice(g * gk, (g + 1) * gk) for g in range(ppb // PAGES_PER_GROUP)]
    update([_dot_nt(q_lat, kc_s[g, :]) + _dot(q_rope, kr_s[:, g]) for g in groups], [kc_s[g, :] for g in groups])

    @pl.when(s_idx == pl.num_programs(1) - 1)
    def _():
        nc_s[...] = jnp.zeros_like(nc_s)
        nc_s[0:NEW_PAD, :] = cn_ref[0].astype(BF16)
        s = _dot_nt(q_lat, nc_s[...]) + _dot(q_rope, kn_ref[0].astype(BF16))
        tok = lax.broadcasted_iota(jnp.int32, s.shape, 0) // MLA_HEADS
        col = lax.broadcasted_iota(jnp.int32, s.shape, 1)
        s = jnp.where((col <= tok) & (col < ts), s, -jnp.inf)
        update([s], [nc_s[...]])
        o_ref[0] = (acc_ref[...] / l_ref[...]).astype(BF16)


def _paged_attention(layer, page_table, q_abs, ckv_new, kr_new_t, cache_ckv, cache_krope_t, ts):
    bs, n_pages = page_table.shape
    page = cache_ckv.shape[2]
    ppb = PAGES_PER_STEP
    rows = ts * MLA_HEADS

    grid_spec = pltpu.PrefetchScalarGridSpec(
        num_scalar_prefetch=1,
        grid=(bs, n_pages // ppb),
        in_specs=[pl.BlockSpec((1, rows, Q_ABS), lambda b, s, pt: (b, 0, 0)),
                  pl.BlockSpec((1, NEW_PAD, KV_LORA), lambda b, s, pt: (b, 0, 0)),
                  pl.BlockSpec((1, ROPE_DIM, LANES), lambda b, s, pt: (b, 0, 0)),
                  pl.BlockSpec(memory_space=pl.ANY), pl.BlockSpec(memory_space=pl.ANY)],
        out_specs=pl.BlockSpec((1, rows, KV_LORA), lambda b, s, pt: (b, 0, 0)),
        scratch_shapes=[pltpu.VMEM((2, ppb, page, KV_LORA), F32), pltpu.VMEM((2, ppb, ROPE_DIM, page), F32),
                        pltpu.SemaphoreType.DMA((2,)), pltpu.SemaphoreType.DMA((2,)),
                        pltpu.VMEM((ppb * page, KV_LORA), BF16), pltpu.VMEM((ROPE_DIM, ppb * page), BF16),
                        pltpu.VMEM((LANES, KV_LORA), BF16),
                        pltpu.VMEM((rows, 1), F32), pltpu.VMEM((rows, 1), F32), pltpu.VMEM((rows, KV_LORA), F32)],
    )
    return pl.pallas_call(
        functools.partial(_paged_kernel, layer, ppb, page, ts),
        grid_spec=grid_spec,
        out_shape=jax.ShapeDtypeStruct((bs, rows, KV_LORA), BF16),
        compiler_params=_cparams(("arbitrary", "arbitrary")),
        name="mla_paged_sample",
    )(page_table, q_abs, ckv_new, kr_new_t, cache_ckv, cache_krope_t)


def _uv_kernel(o_ref, w_ref, out_ref):
    out_ref[...] = _dot(o_ref[...], w_ref[...]).astype(BF16)


def _value_up(o_lat, w_uv):
    n_s = o_lat.shape[0]
    return pl.pallas_call(
        _uv_kernel,
        grid=(MLA_HEADS,),
        in_specs=[pl.BlockSpec((n_s, KV_LORA), lambda h: (0, h)),
                  pl.BlockSpec((KV_LORA, V_DIM), lambda h: (0, h))],
        out_specs=pl.BlockSpec((n_s, V_DIM), lambda h: (0, h)),
        out_shape=jax.ShapeDtypeStruct((n_s, MLA_HEADS * V_DIM), BF16),
        compiler_params=_cparams(("parallel",)),
        name="mla_value_up",
    )(o_lat, w_uv)


def _gdn_post(y, ba, al, q_ref, k_ref, v_ref, gb_ref):
    y = _silu(y)
    for h in range(GDN_HEADS):
        hs = slice(h * GDN_DK, (h + 1) * GDN_DK)
        qh = y[:, hs]
        kh = y[:, GDN_QK + h * GDN_DK:GDN_QK + (h + 1) * GDN_DK]
        q_ref[:, hs] = qh * lax.rsqrt(jnp.sum(qh * qh, axis=-1, keepdims=True) + EPS) * (GDN_DK ** -0.5)
        k_ref[:, hs] = kh * lax.rsqrt(jnp.sum(kh * kh, axis=-1, keepdims=True) + EPS)
    v_ref[...] = y[:, 2 * GDN_QK:]
    xa = ba + al[1:2, :]
    softplus = jnp.maximum(xa, 0.0) + jnp.log(1.0 + jnp.exp(-jnp.abs(xa)))
    lane = lax.broadcasted_iota(jnp.int32, ba.shape, 1)
    gb_ref[...] = jnp.where(lane < GDN_HEADS, _sigmoid(ba), -jnp.exp(al[0:1, :]) * softplus)


def _gdn_prep_prompt_kernel(x_ref, xp_ref, w_ref, ba_ref, al_ref, q_ref, k_ref, v_ref, gb_ref, s_ref):
    tt = x_ref.shape[0]
    s_ref[SUBLANES:, :] = x_ref[...]
    s_ref[0:SUBLANES, :] = jnp.where(pl.program_id(1) > 0, xp_ref[...], 0.0)
    off = SUBLANES - (SHORT_CONV - 1)
    y = jnp.zeros((tt, GDN_QKV), F32)
    for j in range(SHORT_CONV):
        y = y + s_ref[off + j:off + j + tt, :] * w_ref[j:j + 1, :]
    _gdn_post(y, ba_ref[...], al_ref[...], q_ref, k_ref, v_ref, gb_ref)


def _gdn_prep_prompt(act, bp, tp, conv_w, al):
    tt = 128
    nt = tp // tt
    r = tt // SUBLANES
    cq = COL_QKV // GDN_QKV
    row = lambda b, i: (b * nt + i, 0)
    return pl.pallas_call(
        _gdn_prep_prompt_kernel,
        grid=(bp, nt),
        in_specs=[pl.BlockSpec((tt, GDN_QKV), lambda b, i: (b * nt + i, cq)),
                  pl.BlockSpec((SUBLANES, GDN_QKV), lambda b, i: (jnp.maximum((b * nt + i) * r - 1, 0), cq)),
                  pl.BlockSpec((SHORT_CONV, GDN_QKV), lambda b, i: (0, 0)),
                  pl.BlockSpec((tt, LANES), lambda b, i: (b * nt + i, COL_BA // LANES)),
                  pl.BlockSpec((2, LANES), lambda b, i: (0, 0))],
        out_specs=[pl.BlockSpec((tt, GDN_QK), row), pl.BlockSpec((tt, GDN_QK), row),
                   pl.BlockSpec((tt, GDN_QK), row), pl.BlockSpec((tt, LANES), row)],
        out_shape=[jax.ShapeDtypeStruct((bp * tp, GDN_QK), F32)] * 3 + [jax.ShapeDtypeStruct((bp * tp, LANES), F32)],
        scratch_shapes=[pltpu.VMEM((tt + SUBLANES, GDN_QKV), F32)],
        compiler_params=_cparams(("parallel", "arbitrary")),
        name="gdn_prep_prompt",
    )(act, act, conv_w, act, al)


def _gdn_prep_sample_kernel(ts, x_ref, buf_ref, w_ref, ba_ref, al_ref, q_ref, k_ref, v_ref, gb_ref, s_ref, y_ref, g_ref):
    nb = SHORT_CONV - 1
    y_ref[...] = jnp.zeros_like(y_ref)
    g_ref[...] = jnp.zeros_like(g_ref)
    x = x_ref[...]
    ba = ba_ref[...]
    for bi in range(SAMPLE_BB):
        s_ref[0:nb, :] = buf_ref[bi]
        s_ref[nb:nb + ts, :] = x[bi * ts:(bi + 1) * ts, :]
        y = jnp.zeros((ts, GDN_QKV), F32)
        for j in range(SHORT_CONV):
            y = y + s_ref[j:j + ts, :] * w_ref[j:j + 1, :]
        y_ref[bi * NEW_PAD:bi * NEW_PAD + ts, :] = y
        g_ref[bi * NEW_PAD:bi * NEW_PAD + ts, :] = ba[bi * ts:(bi + 1) * ts, :]
    _gdn_post(y_ref[...], g_ref[...], al_ref[...], q_ref, k_ref, v_ref, gb_ref)
    rowi = lax.broadcasted_iota(jnp.int32, gb_ref.shape, 0) % NEW_PAD
    gb_ref[...] = jnp.where(rowi < ts, gb_ref[...], 0.0)


def _gdn_prep_sample(act, n_p, bs, ts, buf, conv_w, al):
    bb = SAMPLE_BB
    rows = bb * ts
    prow = bb * NEW_PAD
    base = n_p // rows
    nb = SHORT_CONV - 1
    row = lambda i: (i, 0)
    return pl.pallas_call(
        functools.partial(_gdn_prep_sample_kernel, ts),
        grid=(bs // bb,),
        in_specs=[pl.BlockSpec((rows, GDN_QKV), lambda i: (base + i, COL_QKV // GDN_QKV)),
                  pl.BlockSpec((bb, nb, GDN_QKV), lambda i: (i, 0, 0)),
                  pl.BlockSpec((SHORT_CONV, GDN_QKV), lambda i: (0, 0)),
                  pl.BlockSpec((rows, LANES), lambda i: (base + i, COL_BA // LANES)),
                  pl.BlockSpec((2, LANES), lambda i: (0, 0))],
        out_specs=[pl.BlockSpec((prow, GDN_QK), row), pl.BlockSpec((prow, GDN_QK), row),
                   pl.BlockSpec((prow, GDN_QK), row), pl.BlockSpec((prow, LANES), row)],
        out_shape=[jax.ShapeDtypeStruct((bs * NEW_PAD, GDN_QK), F32)] * 3
        + [jax.ShapeDtypeStruct((bs * NEW_PAD, LANES), F32)],
        scratch_shapes=[pltpu.VMEM((2 * SUBLANES, GDN_QKV), F32), pltpu.VMEM((prow, GDN_QKV), F32),
                        pltpu.VMEM((prow, LANES), F32)],
        compiler_params=_cparams(("parallel",)),
        name="gdn_prep_sample",
    )(act, buf, conv_w, act, al)


def _split_bf16(a):
    hi = a.astype(BF16)
    return hi, (a - hi.astype(F32)).astype(BF16)


def _dot_x3(a, b):
    (ah, al), (bh, bl) = a, b
    return _dot(ah, bh) + (_dot(ah, bl) + _dot(al, bh))


def _unit_lower_inverses(nmats, eye, n_double):
    ps = [_split_bf16(-nm) for nm in nmats]
    ts = [eye - nm for nm in nmats]
    for _ in range(n_double):
        ps = [_split_bf16(_dot_x3(p, p)) for p in ps]
        ts = [t + _dot_x3(_split_bf16(t), p) for t, p in zip(ts, ps)]
    return ts


def _gated_rmsnorm(o, ng, z):
    return (o * lax.rsqrt(jnp.mean(o * o, axis=-1, keepdims=True) + EPS) * ng * _silu(z)).astype(BF16)


GDN_CHUNKS_PER_STEP = 2


def _gdn_intra_kernel(q_ref, k_ref, v_ref, gb_ref, u_ref, w_ref, qd_ref, kt_ref, at_ref, gam_ref):
    c = GDN_CHUNK
    ri = lax.broadcasted_iota(jnp.int32, (c, c), 0)
    ci = lax.broadcasted_iota(jnp.int32, (c, c), 1)
    eye = (ci == ri).astype(F32)
    rows = lax.broadcasted_iota(jnp.int32, (c, LANES), 0)
    n_double = int(np.log2(c)) - 1
    units, nmats = [], []
    for cc in range(GDN_CHUNKS_PER_STEP):
        rs = slice(cc * c, (cc + 1) * c)
        gb = gb_ref[rs, :]
        g_cum = gb
        step = 1
        while step < c:
            g_cum = g_cum + jnp.where(rows >= step, pltpu.roll(g_cum, step, axis=0), 0.0)
            step *= 2
        g_cum_t = g_cum.T
        for h in range(GDN_HEADS):
            hs = slice(h * GDN_DK, (h + 1) * GDN_DK)
            q = q_ref[rs, hs]
            k = k_ref[rs, hs]
            beta = gb[:, h:h + 1]
            gc = g_cum[:, GDN_HEADS + h:GDN_HEADS + h + 1]
            gr = g_cum_t[GDN_HEADS + h:GDN_HEADS + h + 1, :]
            g_last = gc[c - 1:c, :]
            dec = jnp.exp(jnp.where(ri >= ci, gc - gr, -jnp.inf))
            kb = k.astype(BF16)
            kk = _dot_nt(kb, kb)
            qk = _dot_nt(q.astype(BF16), kb)
            egc = jnp.exp(gc)
            qd_ref[rs, hs] = (q * egc).astype(BF16)
            kt_ref[rs, hs] = (k * jnp.exp(g_last - gc)).astype(BF16)
            at_ref[rs, h * LANES:h * LANES + c] = (qk * dec).astype(BF16)
            at_ref[rs, h * LANES + c:(h + 1) * LANES] = jnp.zeros((c, LANES - c), BF16)
            gam_ref[cc * GDN_HEADS + h:cc * GDN_HEADS + h + 1, :] = jnp.broadcast_to(jnp.exp(g_last), (1, LANES))
            nmats.append(jnp.where(ri > ci, beta * kk * dec, 0.0))
            units.append((rs, hs, beta, beta * egc))
    tinvs = _unit_lower_inverses(nmats, eye, n_double)
    sols = []
    for (rs, hs, beta, beta_egc), tinv in zip(units, tinvs):
        rhs = jnp.concatenate([beta * v_ref[rs, hs], beta_egc * k_ref[rs, hs]], axis=1)
        sols.append(_dot_x3(_split_bf16(tinv), _split_bf16(rhs)))
    for (rs, hs, _, _), sol in zip(units, sols):
        u_ref[rs, hs] = sol[:, :GDN_DV]
        w_ref[rs, hs] = sol[:, GDN_DV:].astype(BF16)


def _gdn_intra(q, k, v, gb):
    n_p = q.shape[0]
    r = GDN_CHUNKS_PER_STEP * GDN_CHUNK
    row = lambda i: (i, 0)
    wide = pl.BlockSpec((r, GDN_QK), row)
    return pl.pallas_call(
        _gdn_intra_kernel,
        grid=(n_p // r,),
        in_specs=[wide, wide, wide, pl.BlockSpec((r, LANES), row)],
        out_specs=[wide, wide, wide, wide, wide, pl.BlockSpec((GDN_CHUNKS_PER_STEP * GDN_HEADS, LANES), row)],
        out_shape=[jax.ShapeDtypeStruct((n_p, GDN_QK), F32)] + [jax.ShapeDtypeStruct((n_p, GDN_QK), BF16)] * 4
        + [jax.ShapeDtypeStruct((n_p // GDN_CHUNK * GDN_HEADS, LANES), F32)],
        compiler_params=_cparams(("parallel",)),
        name="gdn_intra_chunk",
    )(q, k, v, gb)


def _gdn_scan_kernel(u_ref, w_ref, qd_ref, kt_ref, at_ref, gam_ref, z_ref, s0_ref, ng_ref, o_ref, s_ref):
    c = GDN_CHUNK

    @pl.when(pl.program_id(1) == 0)
    def _():
        s_ref[...] = s0_ref[...]

    heads = range(GDN_HEADS)
    hsl = [slice(h * GDN_DK, (h + 1) * GDN_DK) for h in heads]
    ss = [s_ref[0, h] for h in heads]
    sbs = [s.astype(BF16) for s in ss]
    ws = [_dot(w_ref[:, hsl[h]], sbs[h]) for h in heads]
    os_ = [_dot(qd_ref[:, hsl[h]], sbs[h]) for h in heads]
    ubs = [(u_ref[:, hsl[h]] - ws[h]).astype(BF16) for h in heads]
    upd = [_dot_tn(kt_ref[:, hsl[h]], ubs[h]) for h in heads]
    os_ = [os_[h] + _dot(at_ref[:, h * LANES:h * LANES + c], ubs[h]) for h in heads]
    for h in heads:
        s_ref[0, h] = gam_ref[h:h + 1, :] * ss[h] + upd[h]
        o_ref[:, hsl[h]] = _gated_rmsnorm(os_[h], ng_ref[...], z_ref[:, hsl[h]])


def _gdn_scan(u0, w, qd, kt, at, gam, act, s0, norm_g, nseq, nchunk):
    c = GDN_CHUNK
    row = lambda b, n: (b * nchunk + n, 0)
    wide = pl.BlockSpec((c, GDN_QK), row)
    st = pl.BlockSpec((1, GDN_HEADS, GDN_DK, GDN_DV), lambda b, n: (b, 0, 0, 0))
    return pl.pallas_call(
        _gdn_scan_kernel,
        grid=(nseq, nchunk),
        in_specs=[wide, wide, wide, wide, wide, pl.BlockSpec((GDN_HEADS, LANES), row),
                  pl.BlockSpec((c, GDN_QK), lambda b, n: (b * nchunk + n, COL_Z // GDN_QK)),
                  st, pl.BlockSpec((1, GDN_DV), lambda b, n: (0, 0))],
        out_specs=[wide, st],
        out_shape=[jax.ShapeDtypeStruct((nseq * nchunk * c, GDN_QK), BF16),
                   jax.ShapeDtypeStruct((nseq, GDN_HEADS, GDN_DK, GDN_DV), F32)],
        compiler_params=_cparams(("parallel", "arbitrary")),
        name="gdn_scan_chunks",
    )(u0, w, qd, kt, at, gam, act, s0, norm_g.reshape(1, -1))


GDN_SAMPLE_SEQS = 4


def _gdn_sample_kernel(q_ref, k_ref, v_ref, gb_ref, z_ref, s0_ref, ng_ref, o_ref, s_ref):
    c = NEW_PAD
    r = GDN_HEADS * c
    wide = GDN_HEADS * GDN_DK
    ri = lax.broadcasted_iota(jnp.int32, (r, r), 0)
    ci = lax.broadcasted_iota(jnp.int32, (r, r), 1)
    same = (ri // c) == (ci // c)
    incl = same & (ci <= ri)
    strict = same & (ci < ri)
    eye = (ci == ri).astype(F32)
    incl_f = incl.astype(F32)
    head_of_row = lax.broadcasted_iota(jnp.int32, (r, wide), 0) // c
    head_of_lane = lax.broadcasted_iota(jnp.int32, (r, wide), 1) // GDN_DK
    blk = head_of_row == head_of_lane
    n_double = int(np.log2(c)) - 1

    def stack(x):
        return jnp.concatenate([x[:, h * GDN_DK:(h + 1) * GDN_DK] for h in range(GDN_HEADS)], axis=0)

    def spread(x):
        return jnp.where(blk, jnp.concatenate([x] * GDN_HEADS, axis=1), 0.0)

    seqs = range(GDN_SAMPLE_SEQS)
    rsl = [slice(sq * c, (sq + 1) * c) for sq in seqs]
    pre = []
    for sq in seqs:
        gb = gb_ref[rsl[sq], :]
        q, k = stack(q_ref[rsl[sq], :]), stack(k_ref[rsl[sq], :])
        beta = jnp.concatenate([jnp.broadcast_to(gb[:, h:h + 1], (c, LANES)) for h in range(GDN_HEADS)], axis=0)
        g = jnp.concatenate([jnp.broadcast_to(gb[:, GDN_HEADS + h:GDN_HEADS + h + 1], (c, LANES))
                             for h in range(GDN_HEADS)], axis=0)
        g_cum = _dot(incl_f, g, HIGHEST)
        diff = _dot(incl_f, jnp.where(strict, g[:, :r], 0.0), HIGHEST)
        kb = k.astype(BF16)
        pre.append((q, k, beta, g_cum, diff, _dot_nt(kb, kb), _dot_nt(q.astype(BF16), kb)))
    decs = [jnp.exp(jnp.where(incl, p[4], -jnp.inf)) for p in pre]
    tinvs = _unit_lower_inverses([jnp.where(strict, p[2][:, :r] * p[5] * d, 0.0) for p, d in zip(pre, decs)],
                                 eye, n_double)
    egcs = [jnp.exp(p[3]) for p in pre]
    sols = [_dot_x3(_split_bf16(t), _split_bf16(jnp.concatenate(
        [p[2] * stack(v_ref[rsl[sq], :]), (p[2] * e) * p[1]], axis=1)))
        for sq, (p, e, t) in enumerate(zip(pre, egcs, tinvs))]
    ss = [s0_ref[sq].reshape(wide, GDN_DV) for sq in seqs]
    sbs = [s.astype(BF16) for s in ss]
    ws = [_dot(spread(sol[:, GDN_DV:]).astype(BF16), sb) for sol, sb in zip(sols, sbs)]
    os_ = [_dot(spread(p[0] * e).astype(BF16), sb) for p, e, sb in zip(pre, egcs, sbs)]
    ubs = [(sol[:, :GDN_DV] - w).astype(BF16) for sol, w in zip(sols, ws)]
    os_ = [o + _dot((p[6] * d).astype(BF16), ub) for o, p, d, ub in zip(os_, pre, decs, ubs)]
    for sq in seqs:
        g_cum, k = pre[sq][3], pre[sq][1]
        g_last = jnp.concatenate([jnp.broadcast_to(g_cum[h * c + c - 1:h * c + c, :], (c, LANES))
                                  for h in range(GDN_HEADS)], axis=0)
        gamma = jnp.concatenate([jnp.broadcast_to(jnp.exp(g_cum[h * c + c - 1:h * c + c, :]), (GDN_DK, LANES))
                                 for h in range(GDN_HEADS)], axis=0)
        s_new = gamma * ss[sq] + _dot_tn(spread(k * jnp.exp(g_last - g_cum)).astype(BF16), ubs[sq])
        s_ref[sq] = s_new.reshape(GDN_HEADS, GDN_DK, GDN_DV)
        on = _gated_rmsnorm(os_[sq], ng_ref[...], stack(z_ref[rsl[sq], :]))
        for h in range(GDN_HEADS):
            o_ref[rsl[sq], h * GDN_DK:(h + 1) * GDN_DK] = on[h * c:(h + 1) * c, :]


def _gdn_sample(q, k, v, gb, z, s0, norm_g):
    bs = s0.shape[0]
    sq = GDN_SAMPLE_SEQS
    row = lambda i: (i, 0)
    wide = pl.BlockSpec((sq * NEW_PAD, GDN_QK), row)
    st = pl.BlockSpec((sq, GDN_HEADS, GDN_DK, GDN_DV), lambda i: (i, 0, 0, 0))
    return pl.pallas_call(
        _gdn_sample_kernel,
        grid=(bs // sq,),
        in_specs=[wide, wide, wide, pl.BlockSpec((sq * NEW_PAD, LANES), row), wide, st,
                  pl.BlockSpec((1, GDN_DV), lambda i: (0, 0))],
        out_specs=[wide, st],
        out_shape=[jax.ShapeDtypeStruct((bs * NEW_PAD, GDN_QK), BF16),
                   jax.ShapeDtypeStruct((bs, GDN_HEADS, GDN_DK, GDN_DV), F32)],
        compiler_params=_cparams(("parallel",)),
        name="gdn_sample_chunk",
    )(q, k, v, gb, z, s0, norm_g.reshape(1, -1))


def _merge_kernel(c_ref, m_ref, g_ref, wc_ref, wm_ref, wg_ref, g0_ref, g1_ref, g2_ref, o_ref):
    y = (_sigmoid(g0_ref[...]) * _dot(c_ref[...], wc_ref[...])
         + _sigmoid(g1_ref[...]) * _dot(m_ref[...], wm_ref[...])
         + _sigmoid(g2_ref[...]) * _dot(g_ref[...], wg_ref[...]))
    o_ref[...] = y.astype(BF16)


def _merge(c_act, o_mla, o_gdn, w_pw, w_mo, w_go, act):
    n = c_act.shape[0]
    tm = _row_tile(n, 512)
    tn = 512
    nj = D_MODEL // tn
    a = pl.BlockSpec((tm, CONV_CH), lambda j, i: (i, 0))
    w = pl.BlockSpec((CONV_CH, tn), lambda j, i: (0, j))

    def gate(br):
        return pl.BlockSpec((tm, tn), lambda j, i: (i, br * nj + j))

    return pl.pallas_call(
        _merge_kernel,
        grid=(nj, n // tm),
        in_specs=[a, a, a, w, w, w, gate(0), gate(1), gate(2)],
        out_specs=pl.BlockSpec((tm, tn), lambda j, i: (i, j)),
        out_shape=jax.ShapeDtypeStruct((n, D_MODEL), BF16),
        compiler_params=_cparams(("parallel", "parallel")),
        name="merge_branches",
    )(c_act, o_mla, o_gdn, w_pw, w_mo, w_go, act, act, act)


def _out_ln_kernel(alpha, m_ref, w_ref, x_ref, g_ref, b_ref, o_ref):
    y = alpha * x_ref[...] + _dot(m_ref[...], w_ref[...])
    o_ref[...] = _ln_rows(y, g_ref[...], b_ref[...])


def _out_ln(merged, w_out, xt, ln_g, ln_b, alpha):
    n = xt.shape[0]
    tm = ROW_TILE
    vec = pl.BlockSpec((1, D_MODEL), lambda i: (0, 0))
    return pl.pallas_call(
        functools.partial(_out_ln_kernel, alpha),
        grid=(n // tm,),
        in_specs=[pl.BlockSpec((tm, D_MODEL), lambda i: (i, 0)),
                  pl.BlockSpec((D_MODEL, D_MODEL), lambda i: (0, 0)),
                  pl.BlockSpec((tm, D_MODEL), lambda i: (i, 0)), vec, vec],
        out_specs=pl.BlockSpec((tm, D_MODEL), lambda i: (i, 0)),
        out_shape=jax.ShapeDtypeStruct((n, D_MODEL), F32),
        compiler_params=_cparams(("parallel",)),
        name="out_proj_ln1",
    )(merged, w_out, xt, ln_g.reshape(1, -1), ln_b.reshape(1, -1))


def _router_kernel(x_ref, w_ref, b_ref, e_ref, p_ref):
    logits = _dot(x_ref[...], w_ref[...], HIGHEST) + b_ref[...]
    lane = lax.broadcasted_iota(jnp.int32, logits.shape, 1).astype(F32)
    big = jnp.float32(1e9)
    lg = jnp.where(lane < N_GROUPS, logits, -jnp.inf)
    mg = jnp.max(lg, axis=-1, keepdims=True)
    g_val = 1.0 / jnp.sum(jnp.exp(lg - mg), axis=-1, keepdims=True)
    g_idx = jnp.min(jnp.where(lg == mg, lane, big), axis=-1, keepdims=True)
    lo = N_GROUPS + g_idx * EXPERTS_PER_GROUP
    le = jnp.where((lane >= lo) & (lane < lo + EXPERTS_PER_GROUP), logits, -jnp.inf)
    m1 = jnp.max(le, axis=-1, keepdims=True)
    i1 = jnp.min(jnp.where(le == m1, lane, big), axis=-1, keepdims=True)
    le2 = jnp.where(lane == i1, -jnp.inf, le)
    m2 = jnp.max(le2, axis=-1, keepdims=True)
    i2 = jnp.min(jnp.where(le2 == m2, lane, big), axis=-1, keepdims=True)
    se = jnp.sum(jnp.exp(le - m1), axis=-1, keepdims=True)
    p1 = 1.0 / se
    p2 = jnp.exp(m2 - m1) / se
    w1 = g_val * (p1 / (p1 + p2))
    w2 = g_val * (p2 / (p1 + p2))
    e_ref[...] = jnp.where(lane == 0, i1 - N_GROUPS, jnp.where(lane == 1, i2 - N_GROUPS, 0.0)).astype(jnp.int32)
    p_ref[...] = jnp.where(lane == 0, w1, jnp.where(lane == 1, w2, 0.0))


def _router(x1, w_r, b_r):
    n = x1.shape[0]
    tm = ROW_TILE
    return pl.pallas_call(
        _router_kernel,
        grid=(n // tm,),
        in_specs=[pl.BlockSpec((tm, D_MODEL), lambda i: (i, 0)),
                  pl.BlockSpec((D_MODEL, LANES), lambda i: (0, 0)),
                  pl.BlockSpec((1, LANES), lambda i: (0, 0))],
        out_specs=[pl.BlockSpec((tm, LANES), lambda i: (i, 0)), pl.BlockSpec((tm, LANES), lambda i: (i, 0))],
        out_shape=[jax.ShapeDtypeStruct((n, LANES), jnp.int32), jax.ShapeDtypeStruct((n, LANES), F32)],
        compiler_params=_cparams(("parallel",)),
        name="moe_router",
    )(x1, w_r, b_r)


def _gather_rows(src_hbm, dst, sem, idx_ref, base, count):
    def body(r, carry):
        pltpu.make_async_copy(src_hbm.at[pl.ds(idx_ref[base + r], 1)], dst.at[pl.ds(r, 1)], sem).start()
        return carry
    lax.fori_loop(0, count, body, 0, unroll=8)


def _wait_rows(src_hbm, dst, sem, count):
    pltpu.make_async_copy(src_hbm.at[pl.ds(0, count)], dst.at[pl.ds(0, count)], sem).wait()


def _moe_kernel(layer, te_ref, src_ref, nu_ref, x_hbm, rw_ref, wg_ref, wu_ref, wd_ref, y_ref,
                xbuf, sem, wgb, wub, wdb):
    t = pl.program_id(0)
    nt = pl.num_programs(0)
    n_used = nu_ref[0]
    slot = t % 2
    te = MOE_TILE

    @pl.when(t == 0)
    def _():
        _gather_rows(x_hbm, xbuf.at[0], sem.at[0], src_ref, 0, te)

    @pl.when(t + 1 < n_used)
    def _():
        _gather_rows(x_hbm, xbuf.at[1 - slot], sem.at[1 - slot], src_ref, (t + 1) * te, te)

    changed = jnp.logical_or(t == 0, te_ref[t] != te_ref[jnp.maximum(t - 1, 0)])

    @pl.when(jnp.logical_and(changed, t < n_used))
    def _():
        wgb[...] = wg_ref[...].astype(BF16)
        wub[...] = wu_ref[...].astype(BF16)
        wdb[...] = wd_ref[...].astype(BF16)

    @pl.when(jnp.logical_or(t < n_used, t == 0))
    def _():
        _wait_rows(x_hbm, xbuf.at[slot], sem.at[slot], te)

    @pl.when(t < n_used)
    def _():
        xb = xbuf[slot].astype(BF16)
        h = _silu(_dot(xb, wgb[...])) * _dot(xb, wub[...]) * rw_ref[...]
        y_ref[...] = _dot(h.astype(BF16), wdb[...])

    @pl.when(t >= n_used)
    def _():
        y_ref[...] = jnp.zeros_like(y_ref)


def _moe_experts(layer, tile_expert, row_src, n_used, x1, row_w, w_gate, w_up, w_down):
    n_tiles = tile_expert.shape[0]
    te = MOE_TILE
    grid_spec = pltpu.PrefetchScalarGridSpec(
        num_scalar_prefetch=3,
        grid=(n_tiles,),
        in_specs=[pl.BlockSpec(memory_space=pl.ANY),
                  pl.BlockSpec((te, 1), lambda t, te_r, s_r, n_r: (t, 0)),
                  pl.BlockSpec((None, None, D_MODEL, EXPERT_FF), lambda t, te_r, s_r, n_r: (layer, te_r[t], 0, 0)),
                  pl.BlockSpec((None, None, D_MODEL, EXPERT_FF), lambda t, te_r, s_r, n_r: (layer, te_r[t], 0, 0)),
                  pl.BlockSpec((None, None, EXPERT_FF, D_MODEL), lambda t, te_r, s_r, n_r: (layer, te_r[t], 0, 0))],
        out_specs=pl.BlockSpec((te, D_MODEL), lambda t, te_r, s_r, n_r: (t, 0)),
        scratch_shapes=[pltpu.VMEM((2, te, D_MODEL), F32), pltpu.SemaphoreType.DMA((2,)),
                        pltpu.VMEM((D_MODEL, EXPERT_FF), BF16), pltpu.VMEM((D_MODEL, EXPERT_FF), BF16),
                        pltpu.VMEM((EXPERT_FF, D_MODEL), BF16)],
    )
    return pl.pallas_call(
        functools.partial(_moe_kernel, layer),
        grid_spec=grid_spec,
        out_shape=jax.ShapeDtypeStruct((n_tiles * te, D_MODEL), F32),
        compiler_params=_cparams(("arbitrary",)),
        name="moe_experts",
    )(tile_expert, row_src, n_used, x1, row_w, w_gate, w_up, w_down)


def _combine_ln_kernel(alpha, pos_ref, y_hbm, x_ref, g_ref, b_ref, o_ref, ybuf, sem):
    t = pl.program_id(0)
    nt = pl.num_programs(0)
    tm = x_ref.shape[0]
    slot = t % 2

    @pl.when(t == 0)
    def _():
        _gather_rows(y_hbm, ybuf.at[0], sem.at[0], pos_ref, 0, 2 * tm)

    @pl.when(t + 1 < nt)
    def _():
        _gather_rows(y_hbm, ybuf.at[1 - slot], sem.at[1 - slot], pos_ref, (t + 1) * 2 * tm, 2 * tm)

    _wait_rows(y_hbm, ybuf.at[slot], sem.at[slot], 2 * tm)
    y = alpha * x_ref[...] + ybuf[slot, 0:tm, :] + ybuf[slot, tm:2 * tm, :]
    o_ref[...] = _ln_rows(y, g_ref[...], b_ref[...])


def _combine_ln(pos, y_sorted, x1, ln_g, ln_b, alpha):
    n = x1.shape[0]
    tm = ROW_TILE
    vec = pl.BlockSpec((1, D_MODEL), lambda i, p: (0, 0))
    grid_spec = pltpu.PrefetchScalarGridSpec(
        num_scalar_prefetch=1,
        grid=(n // tm,),
        in_specs=[pl.BlockSpec(memory_space=pl.ANY),
                  pl.BlockSpec((tm, D_MODEL), lambda i, p: (i, 0)), vec, vec],
        out_specs=pl.BlockSpec((tm, D_MODEL), lambda i, p: (i, 0)),
        scratch_shapes=[pltpu.VMEM((2, 2 * tm, D_MODEL), F32), pltpu.SemaphoreType.DMA((2,))],
    )
    return pl.pallas_call(
        functools.partial(_combine_ln_kernel, alpha),
        grid_spec=grid_spec,
        out_shape=jax.ShapeDtypeStruct((n, D_MODEL), F32),
        compiler_params=_cparams(("arbitrary",)),
        name="moe_combine_ln2",
    )(pos, y_sorted, x1, ln_g.reshape(1, -1), ln_b.reshape(1, -1))


def _routing_tables(eid, wts, n):
    te = MOE_TILE
    n_tiles = (2 * n) // te + N_EXPERTS
    e = eid.reshape(-1)
    onehot = (e[:, None] == jnp.arange(N_EXPERTS, dtype=jnp.int32)[None, :]).astype(jnp.int32)
    rank = jnp.sum((jnp.cumsum(onehot, axis=0) - onehot) * onehot, axis=1)
    counts = jnp.sum(onehot, axis=0)
    tiles_per = (counts + te - 1) // te
    tile_end = jnp.cumsum(tiles_per)
    tile_start = tile_end - tiles_per
    dest = tile_start[e] * te + rank
    n_used = tile_end[-1:].astype(jnp.int32)
    tile_expert = jnp.minimum(jnp.searchsorted(tile_end, jnp.arange(n_tiles, dtype=jnp.int32), side="right"),
                              N_EXPERTS - 1).astype(jnp.int32)
    token = jnp.arange(2 * n, dtype=jnp.int32) // 2
    row_src = jnp.zeros((n_tiles * te,), jnp.int32).at[dest].set(token)
    row_w = jnp.zeros((n_tiles * te,), F32).at[dest].set(wts.reshape(-1))
    tm = ROW_TILE
    pos = dest.reshape(n // tm, tm, 2).transpose(0, 2, 1).reshape(-1).astype(jnp.int32)
    return tile_expert, row_src, row_w.reshape(-1, 1), n_used, pos


def _rot_cols(w):
    half = ROPE_DIM // 2
    return jnp.concatenate([-w[..., half:], w[..., :half]], axis=-1)


def _pack_w_in_t(w_in):
    d = w_in.shape[0]
    wt = jnp.swapaxes(w_in, 0, 1)
    offs = np.cumsum([0, 2 * CONV_CH, Q_LORA, KV_LORA, ROPE_DIM, GDN_QKV, GDN_HEADS * GDN_DV, GDN_HEADS, GDN_HEADS,
                      N_BRANCHES * D_MODEL])
    seg = [wt[offs[i]:offs[i + 1]] for i in range(9)]
    glu, cq, ckv, kr, qkv, z, b_raw, a_raw, gate = seg
    half = ROPE_DIM // 2
    kr_rot = jnp.concatenate([-kr[half:], kr[:half]], axis=0)
    ba = jnp.concatenate([b_raw, a_raw, jnp.zeros((LANES - 2 * GDN_HEADS, d), w_in.dtype)], axis=0)
    rows = [gate, qkv, glu[:CONV_CH], glu[CONV_CH:], z, cq, ckv, kr, kr_rot, ba,
            jnp.zeros((N_PACK - COL_BA - LANES, d), w_in.dtype)]
    return jnp.concatenate(rows, axis=0)


def _pack_w_uq(w_uq):
    w = w_uq.reshape(Q_LORA, MLA_HEADS, NOPE_DIM + ROPE_DIM)
    nope, rope = w[..., :NOPE_DIM], w[..., NOPE_DIM:]
    zpad = jnp.zeros((Q_LORA, MLA_HEADS, HEAD_QK - NOPE_DIM - ROPE_DIM), w.dtype)
    wa = jnp.concatenate([nope, rope, zpad], axis=-1)
    wb = jnp.concatenate([jnp.zeros_like(nope), _rot_cols(rope), zpad], axis=-1)
    return wa.reshape(Q_LORA, -1).astype(BF16), wb.reshape(Q_LORA, -1).astype(BF16)


def _rope_tables(pos, scale):
    half = ROPE_DIM // 2
    inv_freq = ROPE_THETA ** (-jnp.arange(half, dtype=F32) / half)
    ang = pos.astype(F32)[:, None] * inv_freq[None, :]
    cos = jnp.concatenate([jnp.cos(ang), jnp.cos(ang)], axis=-1)
    sin = jnp.concatenate([jnp.sin(ang), jnp.sin(ang)], axis=-1)
    n = pos.shape[0]
    zpad = jnp.zeros((n, HEAD_QK - NOPE_DIM - ROPE_DIM), F32)
    tab_a = jnp.concatenate([jnp.full((n, NOPE_DIM), scale, F32), scale * cos, zpad], axis=-1)
    tab_b = jnp.concatenate([jnp.zeros((n, NOPE_DIM), F32), scale * sin, zpad], axis=-1)
    tab_cs = jnp.concatenate([cos, sin], axis=-1)
    return tab_a, tab_b, tab_cs


def kernel(x_prompt, x_sample, cache_ckv, cache_krope, state_conf_conv, state_gdn_conv, state_gdn, page_table, w_in, conv_dw, conv_dw_b, conv_ln_g, conv_ln_b, conv_pw, mla_q_norm, mla_w_uq, mla_kv_norm, mla_w_uk, mla_w_uv, mla_w_o, gdn_conv_w, gdn_a_log, gdn_dt_bias, gdn_norm, gdn_w_o, w_out, ln1_g, ln1_b, router_group_w, router_group_b, router_expert_w, router_expert_b, moe_w_gate, moe_w_up, moe_w_down, ln2_g, ln2_b):
    bp, tp, d = x_prompt.shape
    bs, ts, _ = x_sample.shape
    depth = w_in.shape[0]
    n_pages, page = page_table.shape[1], cache_ckv.shape[2]
    past_len = n_pages * page
    n_p, n_s = bp * tp, bs * ts
    n = n_p + n_s
    assert d == D_MODEL and w_in.shape[2] == 2 * CONV_CH + Q_LORA + KV_LORA + ROPE_DIM + GDN_QKV + GDN_HEADS * GDN_DV \
        + 2 * GDN_HEADS + N_BRANCHES * D_MODEL
    assert tp % ATT_TILE == 0 and tp % GDN_CHUNK == 0 and n % ROW_TILE == 0 and n_p % ROW_TILE == 0
    assert ts <= NEW_PAD and bs % SAMPLE_BB == 0 and n_p % (SAMPLE_BB * ts) == 0 and n_pages % PAGES_PER_STEP == 0
    assert (2 * n) % MOE_TILE == 0 and bs % GDN_SAMPLE_SEQS == 0 and tp % (GDN_CHUNKS_PER_STEP * GDN_CHUNK) == 0
    alpha = float((2.0 * depth) ** 0.25)
    scale = float((NOPE_DIM + ROPE_DIM) ** -0.5)

    pos = jnp.concatenate([jnp.tile(jnp.arange(tp, dtype=jnp.int32), bp),
                           jnp.tile(past_len + jnp.arange(ts, dtype=jnp.int32), bs)])
    tab_a, tab_b, tab_cs = _rope_tables(pos, scale)
    e_place = jnp.zeros((LANES, MLA_HEADS, HEAD_QK), F32)
    e_place = e_place.at[jnp.arange(ROPE_DIM)[:, None], jnp.arange(MLA_HEADS)[None, :],
                         NOPE_DIM + jnp.arange(ROPE_DIM)[:, None]].set(1.0).reshape(LANES, -1).astype(BF16)
    zeros_state = jnp.zeros((bp, GDN_HEADS, GDN_DK, GDN_DV), F32)
    cache_krope_t = jnp.swapaxes(cache_krope, 2, 3)

    xt = jnp.concatenate([x_prompt.reshape(n_p, d), x_sample.reshape(n_s, d)], axis=0)
    outs = [[] for _ in range(10)]
    for l in range(depth):
        w_pack = _pack_w_in_t(w_in[l])
        w_qa, w_qb = _pack_w_uq(mla_w_uq[l])
        w_uk = mla_w_uk[l]
        w_uk_pad = jnp.concatenate([w_uk, jnp.zeros((KV_LORA, MLA_HEADS, HEAD_QK - NOPE_DIM), F32)],
                                   axis=-1).reshape(KV_LORA, -1).astype(BF16)
        w_uk_t = jnp.transpose(w_uk, (1, 2, 0)).astype(BF16)
        w_uv = mla_w_uv[l].reshape(KV_LORA, -1).astype(BF16)
        al = jnp.zeros((2, LANES), F32).at[0, GDN_HEADS:2 * GDN_HEADS].set(gdn_a_log[l]) \
            .at[1, GDN_HEADS:2 * GDN_HEADS].set(gdn_dt_bias[l])
        w_r = jnp.concatenate([router_group_w[l], router_expert_w[l],
                               jnp.zeros((d, LANES - N_GROUPS - N_EXPERTS), F32)], axis=1)
        b_r = jnp.concatenate([router_group_b[l], router_expert_b[l],
                               jnp.zeros((LANES - N_GROUPS - N_EXPERTS,), F32)]).reshape(1, LANES)

        act = _inproj(xt, w_pack)

        c_p, u_p = _conf_prompt(act, bp, tp, conv_dw[l], conv_dw_b[l], conv_ln_g[l], conv_ln_b[l])
        c_s, conf_buf_s = _conf_sample(act, n_p, bs, ts, state_conf_conv[l], conv_dw[l], conv_dw_b[l],
                                       conv_ln_g[l], conv_ln_b[l])
        c_act = jnp.concatenate([c_p, c_s], axis=0)
        conf_buf_p = u_p.reshape(bp, tp, CONV_CH)[:, tp - (CONV_WIDTH - 1):]

        q = _mla_q(act, mla_q_norm[l], w_qa, w_qb, tab_a, tab_b)
        ckv, krope, k_full, v_full = _mla_kv(act, mla_kv_norm[l], tab_cs, w_uk_pad, w_uv, e_place)
        o_p = _flash_prompt(q, k_full, v_full, bp, tp)
        q_abs = _q_absorb(q[n_p:], w_uk_t).reshape(bs, ts * MLA_HEADS, Q_ABS)
        ckv_s = ckv[n_p:].reshape(bs, ts, KV_LORA)
        kr_s = krope[n_p:].reshape(bs, ts, ROPE_DIM)
        pad = ((0, 0), (0, NEW_PAD - ts), (0, 0))
        kr_new_t = jnp.pad(jnp.swapaxes(kr_s, 1, 2), ((0, 0), (0, 0), (0, LANES - ts)))
        o_lat = _paged_attention(l, page_table, q_abs, jnp.pad(ckv_s, pad), kr_new_t,
                                 cache_ckv, cache_krope_t, ts)
        o_s = _value_up(o_lat.reshape(n_s, MLA_HEADS * KV_LORA), w_uv)
        o_mla = jnp.concatenate([o_p, o_s], axis=0)

        qg_p, kg_p, vg_p, gb_p = _gdn_prep_prompt(act, bp, tp, gdn_conv_w[l], al)
        og_p, st_p = _gdn_scan(*_gdn_intra(qg_p, kg_p, vg_p, gb_p), act, zeros_state, gdn_norm[l],
                               bp, tp // GDN_CHUNK)
        qg_s, kg_s, vg_s, gb_s = _gdn_prep_sample(act, n_p, bs, ts, state_gdn_conv[l], gdn_conv_w[l], al)
        z_s = jnp.pad(act[n_p:, COL_Z:COL_Z + GDN_QK].reshape(bs, ts, GDN_QK), pad).reshape(bs * NEW_PAD, GDN_QK)
        og_s, st_s = _gdn_sample(qg_s, kg_s, vg_s, gb_s, z_s, state_gdn[l], gdn_norm[l])
        og_s = og_s.reshape(bs, NEW_PAD, GDN_QK)[:, :ts].reshape(n_s, GDN_QK)
        o_gdn = jnp.concatenate([og_p, og_s], axis=0)
        qkv_raw = act[:, COL_QKV:COL_QKV + GDN_QKV]
        gconv_p = qkv_raw[:n_p].reshape(bp, tp, GDN_QKV)[:, tp - (SHORT_CONV - 1):]
        gconv_s = jnp.concatenate([state_gdn_conv[l], qkv_raw[n_p:].reshape(bs, ts, GDN_QKV)],
                                  axis=1)[:, -(SHORT_CONV - 1):]

        merged = _merge(c_act, o_mla, o_gdn, conv_pw[l].astype(BF16), mla_w_o[l].astype(BF16),
                        gdn_w_o[l].astype(BF16), act)
        x1 = _out_ln(merged, w_out[l].astype(BF16), xt, ln1_g[l], ln1_b[l], alpha)

        eid, wts = _router(x1, w_r, b_r)
        tile_expert, row_src, row_w, n_used, pos_tab = _routing_tables(eid[:, :2], wts[:, :2], n)
        y_sorted = _moe_experts(l, tile_expert, row_src, n_used, x1, row_w, moe_w_gate, moe_w_up, moe_w_down)
        xt = _combine_ln(pos_tab, y_sorted, x1, ln2_g[l], ln2_b[l], alpha)

        new = [ckv[:n_p].reshape(bp, tp, KV_LORA), krope[:n_p].reshape(bp, tp, ROPE_DIM), ckv_s, kr_s,
               conf_buf_p, conf_buf_s, gconv_p, gconv_s, st_p, st_s]
        for lst, val in zip(outs, new):
            lst.append(val)

    return (xt[:n_p].reshape(bp, tp, d), xt[n_p:].reshape(bs, ts, d)) + tuple(jnp.stack(o) for o in outs)
```

```python
import functools

import numpy as np
import jax
import jax.numpy as jnp
from jax import lax
from jax.experimental import pallas as pl
from jax.experimental.pallas import tpu as pltpu

F32 = jnp.float32
BF16 = jnp.bfloat16
HIGHEST = lax.Precision.HIGHEST

D_MODEL = 2048
CONV_CH = 1024
CONV_WIDTH = 31
MLA_HEADS = 8
Q_LORA = 512
KV_LORA = 512
NOPE_DIM = 128
ROPE_DIM = 64
V_DIM = 128
ROPE_THETA = 10000.0
GDN_HEADS = 8
GDN_DK = 128
GDN_DV = 128
GDN_QK = GDN_HEADS * GDN_DK
GDN_QKV = 2 * GDN_QK + GDN_HEADS * GDN_DV
SHORT_CONV = 4
GDN_CHUNK = 64
N_GROUPS = 4
EXPERTS_PER_GROUP = 8
N_EXPERTS = N_GROUPS * EXPERTS_PER_GROUP
EXPERT_FF = 512
N_BRANCHES = 3
EPS = 1e-6

LANES = 128
SUBLANES = 8
VMEM_LIMIT_BYTES = 56 * 1024 * 1024

COL_GATE = 0
COL_QKV = 6144
COL_A = 9216
COL_G = 10240
COL_Z = 11264
COL_CQ = 12288
COL_CKV = 12800
COL_KR = 13312
COL_BA = 13440
N_PACK = 13824
TN_IN = 512

HEAD_QK = 256
Q_ABS = 640
MOE_TILE = 256
ROW_TILE = 256


def _cparams(sem):
    return pltpu.CompilerParams(dimension_semantics=sem, vmem_limit_bytes=VMEM_LIMIT_BYTES)


def _dot(a, b, precision=None):
    return jnp.dot(a, b, preferred_element_type=F32, precision=precision)


def _dot_nt(a, b, precision=None):
    return lax.dot_general(a, b, (((1,), (1,)), ((), ())), preferred_element_type=F32, precision=precision)


def _dot_tn(a, b, precision=None):
    return lax.dot_general(a, b, (((0,), (0,)), ((), ())), preferred_element_type=F32, precision=precision)


def _sigmoid(x):
    return 1.0 / (1.0 + jnp.exp(-x))


def _silu(x):
    return x * _sigmoid(x)


def _row_tile(n, cap):
    best = SUBLANES
    for t in range(SUBLANES, cap + 1, SUBLANES):
        if n % t == 0:
            best = t
    return best


def _inproj_kernel(x_ref, w_ref, o_ref):
    o_ref[...] = _dot_nt(x_ref[...], w_ref[...].astype(BF16))


INPROJ_ROWS = 2200


def _inproj(xb, w_pack_t):
    n, d = xb.shape
    tm = _row_tile(n, INPROJ_ROWS)
    return pl.pallas_call(
        _inproj_kernel,
        grid=(n // tm, N_PACK // TN_IN),
        in_specs=[pl.BlockSpec((tm, d), lambda i, j: (i, 0)),
                  pl.BlockSpec((TN_IN, d), lambda i, j: (j, 0))],
        out_specs=pl.BlockSpec((tm, TN_IN), lambda i, j: (i, j)),
        out_shape=jax.ShapeDtypeStruct((n, N_PACK), F32),
        compiler_params=_cparams(("parallel", "arbitrary")),
        name="inproj",
    )(xb, w_pack_t)


def _ln_rows(y, g, b):
    mu = jnp.mean(y, axis=-1, keepdims=True)
    yc = y - mu
    var = jnp.mean(yc * yc, axis=-1, keepdims=True)
    return yc * lax.rsqrt(var + EPS) * g + b


CONV_HALO = 32


def _conf_prompt_kernel(a_ref, g_ref, ap_ref, gp_ref, w_ref, b_ref, lg_ref, lb_ref, act_ref, u_ref, xp_ref, y_ref):
    tt = a_ref.shape[0]
    u = a_ref[...] * _sigmoid(g_ref[...])
    u_ref[...] = u
    xp_ref[CONV_HALO:, :] = u
    up = ap_ref[...] * _sigmoid(gp_ref[...])
    xp_ref[0:CONV_HALO, :] = jnp.where(pl.program_id(1) > 0, up, 0.0)
    off = CONV_HALO - (CONV_WIDTH - 1)
    for c in range(CONV_CH // LANES):
        cs = slice(c * LANES, (c + 1) * LANES)
        acc = jnp.zeros((tt, LANES), F32)
        for j in range(CONV_WIDTH):
            acc = acc + xp_ref[off + j:off + j + tt, cs] * w_ref[j:j + 1, cs]
        y_ref[:, cs] = acc + b_ref[:, cs]
    yn = _ln_rows(y_ref[...], lg_ref[...], lb_ref[...])
    act_ref[...] = _silu(yn).astype(BF16)


def _conf_prompt(act, bp, tp, conv_dw, dw_b, ln_g, ln_b):
    tt = ROW_TILE
    nt = tp // tt
    ca, cg = COL_A // CONV_CH, COL_G // CONV_CH
    r = tt // CONV_HALO

    def cur(col):
        return pl.BlockSpec((tt, CONV_CH), lambda b, i: (b * nt + i, col))

    def prev(col):
        return pl.BlockSpec((CONV_HALO, CONV_CH), lambda b, i: (jnp.maximum((b * nt + i) * r - 1, 0), col))

    vec = pl.BlockSpec((1, CONV_CH), lambda b, i: (0, 0))
    return pl.pallas_call(
        _conf_prompt_kernel,
        grid=(bp, nt),
        in_specs=[cur(ca), cur(cg), prev(ca), prev(cg),
                  pl.BlockSpec((CONV_WIDTH, CONV_CH), lambda b, i: (0, 0)), vec, vec, vec],
        out_specs=[pl.BlockSpec((tt, CONV_CH), lambda b, i: (b * nt + i, 0)),
                   pl.BlockSpec((tt, CONV_CH), lambda b, i: (b * nt + i, 0))],
        out_shape=[jax.ShapeDtypeStruct((bp * tp, CONV_CH), BF16),
                   jax.ShapeDtypeStruct((bp * tp, CONV_CH), F32)],
        scratch_shapes=[pltpu.VMEM((tt + CONV_HALO, CONV_CH), F32), pltpu.VMEM((tt, CONV_CH), F32)],
        compiler_params=_cparams(("parallel", "arbitrary")),
        name="conf_conv_prompt",
    )(act, act, act, act, conv_dw, dw_b.reshape(1, -1), ln_g.reshape(1, -1), ln_b.reshape(1, -1))


SAMPLE_BB = 8


def _conf_sample_kernel(ts, a_ref, g_ref, buf_ref, w_ref, b_ref, lg_ref, lb_ref, act_ref, nbuf_ref, xp_ref, y_ref):
    nb = CONV_WIDTH - 1
    u = a_ref[...] * _sigmoid(g_ref[...])
    w = w_ref[...]
    for bi in range(SAMPLE_BB):
        xp_ref[0:nb, :] = buf_ref[bi]
        xp_ref[nb:nb + ts, :] = u[bi * ts:(bi + 1) * ts, :]
        nbuf_ref[bi] = xp_ref[ts:ts + nb, :]
        for t in range(ts):
            y_ref[bi * ts + t:bi * ts + t + 1, :] = jnp.sum(xp_ref[t:t + CONV_WIDTH, :] * w, axis=0, keepdims=True)
    yn = _ln_rows(y_ref[...] + b_ref[...], lg_ref[...], lb_ref[...])
    act_ref[...] = _silu(yn).astype(BF16)


def _conf_sample(act, n_p, bs, ts, buf, conv_dw, dw_b, ln_g, ln_b):
    bb = SAMPLE_BB
    rows = bb * ts
    nb = CONV_WIDTH - 1
    base = n_p // rows
    vec = pl.BlockSpec((1, CONV_CH), lambda i: (0, 0))
    return pl.pallas_call(
        functools.partial(_conf_sample_kernel, ts),
        grid=(bs // bb,),
        in_specs=[pl.BlockSpec((rows, CONV_CH), lambda i: (base + i, COL_A // CONV_CH)),
                  pl.BlockSpec((rows, CONV_CH), lambda i: (base + i, COL_G // CONV_CH)),
                  pl.BlockSpec((bb, nb, CONV_CH), lambda i: (i, 0, 0)),
                  pl.BlockSpec((CONV_WIDTH, CONV_CH), lambda i: (0, 0)), vec, vec, vec],
        out_specs=[pl.BlockSpec((rows, CONV_CH), lambda i: (i, 0)),
                   pl.BlockSpec((bb, nb, CONV_CH), lambda i: (i, 0, 0))],
        out_shape=[jax.ShapeDtypeStruct((bs * ts, CONV_CH), BF16),
                   jax.ShapeDtypeStruct((bs, nb, CONV_CH), F32)],
        scratch_shapes=[pltpu.VMEM((nb + ts + 6, CONV_CH), F32), pltpu.VMEM((rows, CONV_CH), F32)],
        compiler_params=_cparams(("parallel",)),
        name="conf_conv_sample",
    )(act, act, buf, conv_dw, dw_b.reshape(1, -1), ln_g.reshape(1, -1), ln_b.reshape(1, -1))


def _rms_rows(x, g):
    return x * lax.rsqrt(jnp.mean(x * x, axis=-1, keepdims=True) + EPS) * g


def _mla_q_kernel(cq_ref, g_ref, wa_ref, wb_ref, ca_ref, cb_ref, q_ref):
    cqn = _rms_rows(cq_ref[...], g_ref[...]).astype(BF16)
    qa = _dot(cqn, wa_ref[...])
    qb = _dot(cqn, wb_ref[...])
    ca = ca_ref[...]
    cb = cb_ref[...]
    for h in range(MLA_HEADS):
        hs = slice(h * HEAD_QK, (h + 1) * HEAD_QK)
        q_ref[:, hs] = (qa[:, hs] * ca + qb[:, hs] * cb).astype(BF16)


def _mla_q(act, q_norm, w_qa, w_qb, tab_a, tab_b):
    n = act.shape[0]
    tm = ROW_TILE
    nq = MLA_HEADS * HEAD_QK
    return pl.pallas_call(
        _mla_q_kernel,
        grid=(n // tm,),
        in_specs=[pl.BlockSpec((tm, Q_LORA), lambda i: (i, COL_CQ // Q_LORA)),
                  pl.BlockSpec((1, Q_LORA), lambda i: (0, 0)),
                  pl.BlockSpec((Q_LORA, nq), lambda i: (0, 0)),
                  pl.BlockSpec((Q_LORA, nq), lambda i: (0, 0)),
                  pl.BlockSpec((tm, HEAD_QK), lambda i: (i, 0)),
                  pl.BlockSpec((tm, HEAD_QK), lambda i: (i, 0))],
        out_specs=pl.BlockSpec((tm, nq), lambda i: (i, 0)),
        out_shape=jax.ShapeDtypeStruct((n, nq), BF16),
        compiler_params=_cparams(("parallel",)),
        name="mla_q",
    )(act, q_norm.reshape(1, -1), w_qa, w_qb, tab_a, tab_b)


def _mla_kv_kernel(ckv_ref, krr_ref, g_ref, cs_ref, wuk_ref, wuv_ref, e_ref, ckv_out, kr_out, k_out, v_out):
    ckvn = _rms_rows(ckv_ref[...], g_ref[...])
    ckv_out[...] = ckvn
    t = krr_ref[...] * cs_ref[...]
    kr2 = t + pltpu.roll(t, ROPE_DIM, axis=1)
    kr_out[...] = kr2[:, :ROPE_DIM]
    cb = ckvn.astype(BF16)
    k_out[...] = (_dot(cb, wuk_ref[...]) + _dot(kr2.astype(BF16), e_ref[...])).astype(BF16)
    v_out[...] = _dot(cb, wuv_ref[...]).astype(BF16)


def _mla_kv(act, kv_norm, tab_cs, w_uk_pad, w_uv, e_place):
    n = act.shape[0]
    tm = ROW_TILE
    nk = MLA_HEADS * HEAD_QK
    nv = MLA_HEADS * V_DIM
    return pl.pallas_call(
        _mla_kv_kernel,
        grid=(n // tm,),
        in_specs=[pl.BlockSpec((tm, KV_LORA), lambda i: (i, COL_CKV // KV_LORA)),
                  pl.BlockSpec((tm, LANES), lambda i: (i, COL_KR // LANES)),
                  pl.BlockSpec((1, KV_LORA), lambda i: (0, 0)),
                  pl.BlockSpec((tm, LANES), lambda i: (i, 0)),
                  pl.BlockSpec((KV_LORA, nk), lambda i: (0, 0)),
                  pl.BlockSpec((KV_LORA, nv), lambda i: (0, 0)),
                  pl.BlockSpec((LANES, nk), lambda i: (0, 0))],
        out_specs=[pl.BlockSpec((tm, KV_LORA), lambda i: (i, 0)),
                   pl.BlockSpec((tm, ROPE_DIM), lambda i: (i, 0)),
                   pl.BlockSpec((tm, nk), lambda i: (i, 0)),
                   pl.BlockSpec((tm, nv), lambda i: (i, 0))],
        out_shape=[jax.ShapeDtypeStruct((n, KV_LORA), F32),
                   jax.ShapeDtypeStruct((n, ROPE_DIM), F32),
                   jax.ShapeDtypeStruct((n, nk), BF16),
                   jax.ShapeDtypeStruct((n, nv), BF16)],
        compiler_params=_cparams(("parallel",)),
        name="mla_kv",
    )(act, act, kv_norm.reshape(1, -1), tab_cs, w_uk_pad, w_uv, e_place)


ATT_TILE = 512


ATT_SUB = 128


def _flash_kernel(qi_ref, ki_ref, q_ref, k_ref, v_ref, o_ref, m_ref, l_ref, acc_ref):
    pair = pl.program_id(2)
    qi = qi_ref[pair]
    ki = ki_ref[pair]
    t = q_ref.shape[0]
    rep = t // V_DIM

    @pl.when(ki == 0)
    def _():
        m_ref[...] = jnp.full_like(m_ref, -jnp.inf)
        l_ref[...] = jnp.zeros_like(l_ref)
        acc_ref[...] = jnp.zeros_like(acc_ref)

    def update(masked):
        k = k_ref[...]
        v = v_ref[...]
        rsl = [slice(sb * ATT_SUB, (sb + 1) * ATT_SUB) for sb in range(t // ATT_SUB)]
        scores = [_dot_nt(q_ref[rs, :], k) for rs in rsl]
        probs = []
        for sb, (rs, s) in enumerate(zip(rsl, scores)):
            if masked:
                rows = lax.broadcasted_iota(jnp.int32, s.shape, 0) + sb * ATT_SUB
                cols = lax.broadcasted_iota(jnp.int32, s.shape, 1)
                s = jnp.where(cols <= rows, s, -jnp.inf)
            m_old = m_ref[rs, :]
            m_new = jnp.maximum(m_old, jnp.broadcast_to(jnp.max(s, axis=-1, keepdims=True), m_old.shape))
            p = jnp.exp(s - jnp.concatenate([m_new] * rep, axis=1))
            alpha = jnp.exp(m_old - m_new)
            l_ref[rs, :] = alpha * l_ref[rs, :] + jnp.broadcast_to(jnp.sum(p, axis=-1, keepdims=True), m_old.shape)
            acc_ref[rs, :] = alpha * acc_ref[rs, :]
            m_ref[rs, :] = m_new
            probs.append(p.astype(BF16))
        for rs, p in zip(rsl, probs):
            acc_ref[rs, :] += _dot(p, v)

    @pl.when(ki < qi)
    def _():
        update(False)

    @pl.when(ki == qi)
    def _():
        update(True)
        o_ref[...] = (acc_ref[...] / l_ref[...]).astype(BF16)


def _flash_prompt(q, k, v, bp, tp):
    t = ATT_TILE
    nt = tp // t
    pairs = [(a, b) for a in range(nt) for b in range(a + 1)]
    qi_list = jnp.asarray([a for a, _ in pairs], jnp.int32)
    ki_list = jnp.asarray([b for _, b in pairs], jnp.int32)
    grid_spec = pltpu.PrefetchScalarGridSpec(
        num_scalar_prefetch=2,
        grid=(bp, MLA_HEADS, len(pairs)),
        in_specs=[pl.BlockSpec((t, HEAD_QK), lambda b, h, p, qi, ki: (b * nt + qi[p], h)),
                  pl.BlockSpec((t, HEAD_QK), lambda b, h, p, qi, ki: (b * nt + ki[p], h)),
                  pl.BlockSpec((t, V_DIM), lambda b, h, p, qi, ki: (b * nt + ki[p], h))],
        out_specs=pl.BlockSpec((t, V_DIM), lambda b, h, p, qi, ki: (b * nt + qi[p], h)),
        scratch_shapes=[pltpu.VMEM((t, V_DIM), F32), pltpu.VMEM((t, V_DIM), F32), pltpu.VMEM((t, V_DIM), F32)],
    )
    return pl.pallas_call(
        _flash_kernel,
        grid_spec=grid_spec,
        out_shape=jax.ShapeDtypeStruct((bp * tp, MLA_HEADS * V_DIM), BF16),
        compiler_params=_cparams(("parallel", "parallel", "arbitrary")),
        name="mla_flash_prompt",
    )(qi_list, ki_list, q, k, v)


def _q_absorb_kernel(q_ref, wt_ref, o_ref):
    q = q_ref[...]
    o_ref[:, :KV_LORA] = _dot(q[:, :NOPE_DIM], wt_ref[...]).astype(BF16)
    o_ref[:, KV_LORA:] = q[:, NOPE_DIM:]


def _q_absorb(q_s, w_uk_t):
    n_s = q_s.shape[0]
    return pl.pallas_call(
        _q_absorb_kernel,
        grid=(MLA_HEADS,),
        in_specs=[pl.BlockSpec((n_s, HEAD_QK), lambda h: (0, h)),
                  pl.BlockSpec((None, NOPE_DIM, KV_LORA), lambda h: (h, 0, 0))],
        out_specs=pl.BlockSpec((n_s, Q_ABS), lambda h: (0, h)),
        out_shape=jax.ShapeDtypeStruct((n_s, MLA_HEADS * Q_ABS), BF16),
        compiler_params=_cparams(("parallel",)),
        name="mla_q_absorb",
    )(q_s, w_uk_t)


PAGES_PER_STEP = 16
PAGES_PER_GROUP = 4
PAGE_SLOTS = 3
NEW_PAD = 8


def _paged_kernel(layer, ppb, page, ts, pt_ref, q_ref, cn_ref, kn_ref, ck_hbm, kr_hbm, o_ref,
                  ckbuf, krbuf, ck_sem, kr_sem, kc_s, kr_s, nc_s, m_ref, l_ref, acc_ref):
    b_idx = pl.program_id(0)
    s_idx = pl.program_id(1)
    n_steps = pl.num_programs(1)
    g = b_idx * n_steps + s_idx
    total = pl.num_programs(0) * n_steps
    slot = g % PAGE_SLOTS
    gk = PAGES_PER_GROUP * page

    def fetch(gi, dst_slot):
        bb = gi // n_steps
        first = (gi % n_steps) * ppb
        for k in range(ppb):
            pid = pt_ref[bb, first + k]
            pltpu.make_async_copy(ck_hbm.at[layer, pid], ckbuf.at[dst_slot, k], ck_sem.at[dst_slot]).start()
            pltpu.make_async_copy(kr_hbm.at[layer, pid], krbuf.at[dst_slot, k], kr_sem.at[dst_slot]).start()

    ahead = PAGE_SLOTS - 1

    @pl.when(g == 0)
    def _():
        for a in range(ahead):
            @pl.when(a < total)
            def _():
                fetch(a, a)

    @pl.when(g + ahead < total)
    def _():
        fetch(g + ahead, (g + ahead) % PAGE_SLOTS)

    pltpu.make_async_copy(ck_hbm.at[layer, pl.ds(0, ppb)], ckbuf.at[slot], ck_sem.at[slot]).wait()
    pltpu.make_async_copy(kr_hbm.at[layer, pl.ds(0, ppb)], krbuf.at[slot], kr_sem.at[slot]).wait()

    @pl.when(s_idx == 0)
    def _():
        m_ref[...] = jnp.full_like(m_ref, -jnp.inf)
        l_ref[...] = jnp.zeros_like(l_ref)
        acc_ref[...] = jnp.zeros_like(acc_ref)

    for k in range(ppb):
        kc_s[k * page:(k + 1) * page, :] = ckbuf[slot, k].astype(BF16)
        kr_s[:, k * page:(k + 1) * page] = krbuf[slot, k].astype(BF16)
    q = q_ref[0]
    q_lat = q[:, :KV_LORA]
    q_rope = q[:, KV_LORA:KV_LORA + ROPE_DIM]

    def update(scores, vals):
        m_old = m_ref[...]
        m_new = m_old
        for s in scores:
            m_new = jnp.maximum(m_new, jnp.max(s, axis=-1, keepdims=True))
        alpha = jnp.exp(m_old - m_new)
        l_new = alpha * l_ref[...]
        acc = alpha * acc_ref[...]
        for s, val in zip(scores, vals):
            p = jnp.exp(s - m_new)
            l_new = l_new + jnp.sum(p, axis=-1, keepdims=True)
            acc = acc + _dot(p.astype(BF16), val)
        l_ref[...] = l_new
        acc_ref[...] = acc
        m_ref[...] = m_new

    groups = [slice(g * gk, (g + 1) * gk) for g in range(ppb // PAGES_PER_GROUP)]
    update([_dot_nt(q_lat, kc_s[g, :]) + _dot(q_rope, kr_s[:, g]) for g in groups], [kc_s[g, :] for g in groups])

    @pl.when(s_idx == pl.num_programs(1) - 1)
    def _():
        nc_s[...] = jnp.zeros_like(nc_s)
        nc_s[0:NEW_PAD, :] = cn_ref[0].astype(BF16)
        s = _dot_nt(q_lat, nc_s[...]) + _dot(q_rope, kn_ref[0].astype(BF16))
        tok = lax.broadcasted_iota(jnp.int32, s.shape, 0) // MLA_HEADS
        col = lax.broadcasted_iota(jnp.int32, s.shape, 1)
        s = jnp.where((col <= tok) & (col < ts), s, -jnp.inf)
        update([s], [nc_s[...]])
        o_ref[0] = (acc_ref[...] / l_ref[...]).astype(BF16)


def _paged_attention(layer, page_table, q_abs, ckv_new, kr_new_t, cache_ckv, cache_krope_t, ts):
    bs, n_pages = page_table.shape
    page = cache_ckv.shape[2]
    ppb = PAGES_PER_STEP
    rows = ts * MLA_HEADS

    grid_spec = pltpu.PrefetchScalarGridSpec(
        num_scalar_prefetch=1,
        grid=(bs, n_pages // ppb),
        in_specs=[pl.BlockSpec((1, rows, Q_ABS), lambda b, s, pt: (b, 0, 0)),
                  pl.BlockSpec((1, NEW_PAD, KV_LORA), lambda b, s, pt: (b, 0, 0)),
                  pl.BlockSpec((1, ROPE_DIM, LANES), lambda b, s, pt: (b, 0, 0)),
                  pl.BlockSpec(memory_space=pl.ANY), pl.BlockSpec(memory_space=pl.ANY)],
        out_specs=pl.BlockSpec((1, rows, KV_LORA), lambda b, s, pt: (b, 0, 0)),
        scratch_shapes=[pltpu.VMEM((PAGE_SLOTS, ppb, page, KV_LORA), F32),
                        pltpu.VMEM((PAGE_SLOTS, ppb, ROPE_DIM, page), F32),
                        pltpu.SemaphoreType.DMA((PAGE_SLOTS,)), pltpu.SemaphoreType.DMA((PAGE_SLOTS,)),
                        pltpu.VMEM((ppb * page, KV_LORA), BF16), pltpu.VMEM((ROPE_DIM, ppb * page), BF16),
                        pltpu.VMEM((LANES, KV_LORA), BF16),
                        pltpu.VMEM((rows, 1), F32), pltpu.VMEM((rows, 1), F32), pltpu.VMEM((rows, KV_LORA), F32)],
    )
    return pl.pallas_call(
        functools.partial(_paged_kernel, layer, ppb, page, ts),
        grid_spec=grid_spec,
        out_shape=jax.ShapeDtypeStruct((bs, rows, KV_LORA), BF16),
        compiler_params=_cparams(("arbitrary", "arbitrary")),
        name="mla_paged_sample",
    )(page_table, q_abs, ckv_new, kr_new_t, cache_ckv, cache_krope_t)


def _uv_kernel(o_ref, w_ref, out_ref):
    out_ref[...] = _dot(o_ref[...], w_ref[...]).astype(BF16)


def _value_up(o_lat, w_uv):
    n_s = o_lat.shape[0]
    return pl.pallas_call(
        _uv_kernel,
        grid=(MLA_HEADS,),
        in_specs=[pl.BlockSpec((n_s, KV_LORA), lambda h: (0, h)),
                  pl.BlockSpec((KV_LORA, V_DIM), lambda h: (0, h))],
        out_specs=pl.BlockSpec((n_s, V_DIM), lambda h: (0, h)),
        out_shape=jax.ShapeDtypeStruct((n_s, MLA_HEADS * V_DIM), BF16),
        compiler_params=_cparams(("parallel",)),
        name="mla_value_up",
    )(o_lat, w_uv)


def _gdn_post(y, ba, al, q_ref, k_ref, v_ref, gb_ref):
    y = _silu(y)
    for h in range(GDN_HEADS):
        hs = slice(h * GDN_DK, (h + 1) * GDN_DK)
        qh = y[:, hs]
        kh = y[:, GDN_QK + h * GDN_DK:GDN_QK + (h + 1) * GDN_DK]
        q_ref[:, hs] = qh * lax.rsqrt(jnp.sum(qh * qh, axis=-1, keepdims=True) + EPS) * (GDN_DK ** -0.5)
        k_ref[:, hs] = kh * lax.rsqrt(jnp.sum(kh * kh, axis=-1, keepdims=True) + EPS)
    v_ref[...] = y[:, 2 * GDN_QK:]
    xa = ba + al[1:2, :]
    softplus = jnp.maximum(xa, 0.0) + jnp.log(1.0 + jnp.exp(-jnp.abs(xa)))
    lane = lax.broadcasted_iota(jnp.int32, ba.shape, 1)
    gb_ref[...] = jnp.where(lane < GDN_HEADS, _sigmoid(ba), -jnp.exp(al[0:1, :]) * softplus)


def _gdn_prep_prompt_kernel(x_ref, xp_ref, w_ref, ba_ref, al_ref, q_ref, k_ref, v_ref, gb_ref, s_ref):
    tt = x_ref.shape[0]
    s_ref[SUBLANES:, :] = x_ref[...]
    s_ref[0:SUBLANES, :] = jnp.where(pl.program_id(1) > 0, xp_ref[...], 0.0)
    off = SUBLANES - (SHORT_CONV - 1)
    y = jnp.zeros((tt, GDN_QKV), F32)
    for j in range(SHORT_CONV):
        y = y + s_ref[off + j:off + j + tt, :] * w_ref[j:j + 1, :]
    _gdn_post(y, ba_ref[...], al_ref[...], q_ref, k_ref, v_ref, gb_ref)


def _gdn_prep_prompt(act, bp, tp, conv_w, al):
    tt = 128
    nt = tp // tt
    r = tt // SUBLANES
    cq = COL_QKV // GDN_QKV
    row = lambda b, i: (b * nt + i, 0)
    return pl.pallas_call(
        _gdn_prep_prompt_kernel,
        grid=(bp, nt),
        in_specs=[pl.BlockSpec((tt, GDN_QKV), lambda b, i: (b * nt + i, cq)),
                  pl.BlockSpec((SUBLANES, GDN_QKV), lambda b, i: (jnp.maximum((b * nt + i) * r - 1, 0), cq)),
                  pl.BlockSpec((SHORT_CONV, GDN_QKV), lambda b, i: (0, 0)),
                  pl.BlockSpec((tt, LANES), lambda b, i: (b * nt + i, COL_BA // LANES)),
                  pl.BlockSpec((2, LANES), lambda b, i: (0, 0))],
        out_specs=[pl.BlockSpec((tt, GDN_QK), row), pl.BlockSpec((tt, GDN_QK), row),
                   pl.BlockSpec((tt, GDN_QK), row), pl.BlockSpec((tt, LANES), row)],
        out_shape=[jax.ShapeDtypeStruct((bp * tp, GDN_QK), F32)] * 3 + [jax.ShapeDtypeStruct((bp * tp, LANES), F32)],
        scratch_shapes=[pltpu.VMEM((tt + SUBLANES, GDN_QKV), F32)],
        compiler_params=_cparams(("parallel", "arbitrary")),
        name="gdn_prep_prompt",
    )(act, act, conv_w, act, al)


def _gdn_prep_sample_kernel(ts, x_ref, buf_ref, w_ref, ba_ref, al_ref, q_ref, k_ref, v_ref, gb_ref, s_ref, y_ref, g_ref):
    nb = SHORT_CONV - 1
    y_ref[...] = jnp.zeros_like(y_ref)
    g_ref[...] = jnp.zeros_like(g_ref)
    x = x_ref[...]
    ba = ba_ref[...]
    for bi in range(SAMPLE_BB):
        s_ref[0:nb, :] = buf_ref[bi]
        s_ref[nb:nb + ts, :] = x[bi * ts:(bi + 1) * ts, :]
        y = jnp.zeros((ts, GDN_QKV), F32)
        for j in range(SHORT_CONV):
            y = y + s_ref[j:j + ts, :] * w_ref[j:j + 1, :]
        y_ref[bi * NEW_PAD:bi * NEW_PAD + ts, :] = y
        g_ref[bi * NEW_PAD:bi * NEW_PAD + ts, :] = ba[bi * ts:(bi + 1) * ts, :]
    _gdn_post(y_ref[...], g_ref[...], al_ref[...], q_ref, k_ref, v_ref, gb_ref)
    rowi = lax.broadcasted_iota(jnp.int32, gb_ref.shape, 0) % NEW_PAD
    gb_ref[...] = jnp.where(rowi < ts, gb_ref[...], 0.0)


def _gdn_prep_sample(act, n_p, bs, ts, buf, conv_w, al):
    bb = SAMPLE_BB
    rows = bb * ts
    prow = bb * NEW_PAD
    base = n_p // rows
    nb = SHORT_CONV - 1
    row = lambda i: (i, 0)
    return pl.pallas_call(
        functools.partial(_gdn_prep_sample_kernel, ts),
        grid=(bs // bb,),
        in_specs=[pl.BlockSpec((rows, GDN_QKV), lambda i: (base + i, COL_QKV // GDN_QKV)),
                  pl.BlockSpec((bb, nb, GDN_QKV), lambda i: (i, 0, 0)),
                  pl.BlockSpec((SHORT_CONV, GDN_QKV), lambda i: (0, 0)),
                  pl.BlockSpec((rows, LANES), lambda i: (base + i, COL_BA // LANES)),
                  pl.BlockSpec((2, LANES), lambda i: (0, 0))],
        out_specs=[pl.BlockSpec((prow, GDN_QK), row), pl.BlockSpec((prow, GDN_QK), row),
                   pl.BlockSpec((prow, GDN_QK), row), pl.BlockSpec((prow, LANES), row)],
        out_shape=[jax.ShapeDtypeStruct((bs * NEW_PAD, GDN_QK), F32)] * 3
        + [jax.ShapeDtypeStruct((bs * NEW_PAD, LANES), F32)],
        scratch_shapes=[pltpu.VMEM((2 * SUBLANES, GDN_QKV), F32), pltpu.VMEM((prow, GDN_QKV), F32),
                        pltpu.VMEM((prow, LANES), F32)],
        compiler_params=_cparams(("parallel",)),
        name="gdn_prep_sample",
    )(act, buf, conv_w, act, al)


def _split_bf16(a):
    hi = a.astype(BF16)
    return hi, (a - hi.astype(F32)).astype(BF16)


def _dot_x3(a, b):
    (ah, al), (bh, bl) = a, b
    return _dot(ah, bh) + (_dot(ah, bl) + _dot(al, bh))


def _unit_lower_inverses(nmats, eye, n_double):
    ps = [_split_bf16(-nm) for nm in nmats]
    ts = [eye - nm for nm in nmats]
    for _ in range(n_double):
        ps = [_split_bf16(_dot_x3(p, p)) for p in ps]
        ts = [t + _dot_x3(_split_bf16(t), p) for t, p in zip(ts, ps)]
    return ts


def _gated_rmsnorm(o, ng, z):
    return (o * lax.rsqrt(jnp.mean(o * o, axis=-1, keepdims=True) + EPS) * ng * _silu(z)).astype(BF16)


GDN_CHUNKS_PER_STEP = 2


def _gdn_intra_kernel(q_ref, k_ref, v_ref, gb_ref, u_ref, w_ref, qd_ref, kt_ref, at_ref, gam_ref):
    c = GDN_CHUNK
    ri = lax.broadcasted_iota(jnp.int32, (c, c), 0)
    ci = lax.broadcasted_iota(jnp.int32, (c, c), 1)
    eye = (ci == ri).astype(F32)
    rows = lax.broadcasted_iota(jnp.int32, (c, LANES), 0)
    n_double = int(np.log2(c)) - 1
    units, nmats = [], []
    for cc in range(GDN_CHUNKS_PER_STEP):
        rs = slice(cc * c, (cc + 1) * c)
        gb = gb_ref[rs, :]
        g_cum = gb
        step = 1
        while step < c:
            g_cum = g_cum + jnp.where(rows >= step, pltpu.roll(g_cum, step, axis=0), 0.0)
            step *= 2
        g_cum_t = g_cum.T
        for h in range(GDN_HEADS):
            hs = slice(h * GDN_DK, (h + 1) * GDN_DK)
            q = q_ref[rs, hs]
            k = k_ref[rs, hs]
            beta = gb[:, h:h + 1]
            gc = g_cum[:, GDN_HEADS + h:GDN_HEADS + h + 1]
            gr = g_cum_t[GDN_HEADS + h:GDN_HEADS + h + 1, :]
            g_last = gc[c - 1:c, :]
            dec = jnp.exp(jnp.where(ri >= ci, gc - gr, -jnp.inf))
            kb = k.astype(BF16)
            kk = _dot_nt(kb, kb)
            qk = _dot_nt(q.astype(BF16), kb)
            egc = jnp.exp(gc)
            qd_ref[rs, hs] = (q * egc).astype(BF16)
            kt_ref[rs, hs] = (k * jnp.exp(g_last - gc)).astype(BF16)
            at_ref[rs, h * LANES:h * LANES + c] = (qk * dec).astype(BF16)
            at_ref[rs, h * LANES + c:(h + 1) * LANES] = jnp.zeros((c, LANES - c), BF16)
            gam_ref[cc * GDN_HEADS + h:cc * GDN_HEADS + h + 1, :] = jnp.broadcast_to(jnp.exp(g_last), (1, LANES))
            nmats.append(jnp.where(ri > ci, beta * kk * dec, 0.0))
            units.append((rs, hs, beta, beta * egc))
    tinvs = _unit_lower_inverses(nmats, eye, n_double)
    sols = []
    for (rs, hs, beta, beta_egc), tinv in zip(units, tinvs):
        rhs = jnp.concatenate([beta * v_ref[rs, hs], beta_egc * k_ref[rs, hs]], axis=1)
        sols.append(_dot_x3(_split_bf16(tinv), _split_bf16(rhs)))
    for (rs, hs, _, _), sol in zip(units, sols):
        u_ref[rs, hs] = sol[:, :GDN_DV]
        w_ref[rs, hs] = sol[:, GDN_DV:].astype(BF16)


def _gdn_intra(q, k, v, gb):
    n_p = q.shape[0]
    r = GDN_CHUNKS_PER_STEP * GDN_CHUNK
    row = lambda i: (i, 0)
    wide = pl.BlockSpec((r, GDN_QK), row)
    return pl.pallas_call(
        _gdn_intra_kernel,
        grid=(n_p // r,),
        in_specs=[wide, wide, wide, pl.BlockSpec((r, LANES), row)],
        out_specs=[wide, wide, wide, wide, wide, pl.BlockSpec((GDN_CHUNKS_PER_STEP * GDN_HEADS, LANES), row)],
        out_shape=[jax.ShapeDtypeStruct((n_p, GDN_QK), F32)] + [jax.ShapeDtypeStruct((n_p, GDN_QK), BF16)] * 4
        + [jax.ShapeDtypeStruct((n_p // GDN_CHUNK * GDN_HEADS, LANES), F32)],
        compiler_params=_cparams(("parallel",)),
        name="gdn_intra_chunk",
    )(q, k, v, gb)


def _gdn_scan_kernel(u_ref, w_ref, qd_ref, kt_ref, at_ref, gam_ref, z_ref, s0_ref, ng_ref, o_ref, s_ref):
    c = GDN_CHUNK

    @pl.when(pl.program_id(1) == 0)
    def _():
        s_ref[...] = s0_ref[...]

    heads = range(GDN_HEADS)
    hsl = [slice(h * GDN_DK, (h + 1) * GDN_DK) for h in heads]
    ss = [s_ref[0, h] for h in heads]
    sbs = [s.astype(BF16) for s in ss]
    ws = [_dot(w_ref[:, hsl[h]], sbs[h]) for h in heads]
    os_ = [_dot(qd_ref[:, hsl[h]], sbs[h]) for h in heads]
    ubs = [(u_ref[:, hsl[h]] - ws[h]).astype(BF16) for h in heads]
    upd = [_dot_tn(kt_ref[:, hsl[h]], ubs[h]) for h in heads]
    os_ = [os_[h] + _dot(at_ref[:, h * LANES:h * LANES + c], ubs[h]) for h in heads]
    for h in heads:
        s_ref[0, h] = gam_ref[h:h + 1, :] * ss[h] + upd[h]
        o_ref[:, hsl[h]] = _gated_rmsnorm(os_[h], ng_ref[...], z_ref[:, hsl[h]])


def _gdn_scan(u0, w, qd, kt, at, gam, act, s0, norm_g, nseq, nchunk):
    c = GDN_CHUNK
    row = lambda b, n: (b * nchunk + n, 0)
    wide = pl.BlockSpec((c, GDN_QK), row)
    st = pl.BlockSpec((1, GDN_HEADS, GDN_DK, GDN_DV), lambda b, n: (b, 0, 0, 0))
    return pl.pallas_call(
        _gdn_scan_kernel,
        grid=(nseq, nchunk),
        in_specs=[wide, wide, wide, wide, wide, pl.BlockSpec((GDN_HEADS, LANES), row),
                  pl.BlockSpec((c, GDN_QK), lambda b, n: (b * nchunk + n, COL_Z // GDN_QK)),
                  st, pl.BlockSpec((1, GDN_DV), lambda b, n: (0, 0))],
        out_specs=[wide, st],
        out_shape=[jax.ShapeDtypeStruct((nseq * nchunk * c, GDN_QK), BF16),
                   jax.ShapeDtypeStruct((nseq, GDN_HEADS, GDN_DK, GDN_DV), F32)],
        compiler_params=_cparams(("parallel", "arbitrary")),
        name="gdn_scan_chunks",
    )(u0, w, qd, kt, at, gam, act, s0, norm_g.reshape(1, -1))


GDN_SAMPLE_SEQS = 4


def _gdn_sample_kernel(q_ref, k_ref, v_ref, gb_ref, z_ref, s0_ref, ng_ref, o_ref, s_ref):
    c = NEW_PAD
    r = GDN_HEADS * c
    wide = GDN_HEADS * GDN_DK
    ri = lax.broadcasted_iota(jnp.int32, (r, r), 0)
    ci = lax.broadcasted_iota(jnp.int32, (r, r), 1)
    same = (ri // c) == (ci // c)
    incl = same & (ci <= ri)
    strict = same & (ci < ri)
    eye = (ci == ri).astype(F32)
    incl_f = incl.astype(F32)
    head_of_row = lax.broadcasted_iota(jnp.int32, (r, wide), 0) // c
    head_of_lane = lax.broadcasted_iota(jnp.int32, (r, wide), 1) // GDN_DK
    blk = head_of_row == head_of_lane
    n_double = int(np.log2(c)) - 1

    def stack(x):
        return jnp.concatenate([x[:, h * GDN_DK:(h + 1) * GDN_DK] for h in range(GDN_HEADS)], axis=0)

    def spread(x):
        return jnp.where(blk, jnp.concatenate([x] * GDN_HEADS, axis=1), 0.0)

    seqs = range(GDN_SAMPLE_SEQS)
    rsl = [slice(sq * c, (sq + 1) * c) for sq in seqs]
    pre = []
    for sq in seqs:
        gb = gb_ref[rsl[sq], :]
        q, k = stack(q_ref[rsl[sq], :]), stack(k_ref[rsl[sq], :])
        beta = jnp.concatenate([jnp.broadcast_to(gb[:, h:h + 1], (c, LANES)) for h in range(GDN_HEADS)], axis=0)
        g = jnp.concatenate([jnp.broadcast_to(gb[:, GDN_HEADS + h:GDN_HEADS + h + 1], (c, LANES))
                             for h in range(GDN_HEADS)], axis=0)
        g_cum = _dot(incl_f, g, HIGHEST)
        diff = _dot(incl_f, jnp.where(strict, g[:, :r], 0.0), HIGHEST)
        kb = k.astype(BF16)
        pre.append((q, k, beta, g_cum, diff, _dot_nt(kb, kb), _dot_nt(q.astype(BF16), kb)))
    decs = [jnp.exp(jnp.where(incl, p[4], -jnp.inf)) for p in pre]
    tinvs = _unit_lower_inverses([jnp.where(strict, p[2][:, :r] * p[5] * d, 0.0) for p, d in zip(pre, decs)],
                                 eye, n_double)
    egcs = [jnp.exp(p[3]) for p in pre]
    sols = [_dot_x3(_split_bf16(t), _split_bf16(jnp.concatenate(
        [p[2] * stack(v_ref[rsl[sq], :]), (p[2] * e) * p[1]], axis=1)))
        for sq, (p, e, t) in enumerate(zip(pre, egcs, tinvs))]
    ss = [s0_ref[sq].reshape(wide, GDN_DV) for sq in seqs]
    sbs = [s.astype(BF16) for s in ss]
    ws = [_dot(spread(sol[:, GDN_DV:]).astype(BF16), sb) for sol, sb in zip(sols, sbs)]
    os_ = [_dot(spread(p[0] * e).astype(BF16), sb) for p, e, sb in zip(pre, egcs, sbs)]
    ubs = [(sol[:, :GDN_DV] - w).astype(BF16) for sol, w in zip(sols, ws)]
    os_ = [o + _dot((p[6] * d).astype(BF16), ub) for o, p, d, ub in zip(os_, pre, decs, ubs)]
    for sq in seqs:
        g_cum, k = pre[sq][3], pre[sq][1]
        g_last = jnp.concatenate([jnp.broadcast_to(g_cum[h * c + c - 1:h * c + c, :], (c, LANES))
                                  for h in range(GDN_HEADS)], axis=0)
        gamma = jnp.concatenate([jnp.broadcast_to(jnp.exp(g_cum[h * c + c - 1:h * c + c, :]), (GDN_DK, LANES))
                                 for h in range(GDN_HEADS)], axis=0)
        s_new = gamma * ss[sq] + _dot_tn(spread(k * jnp.exp(g_last - g_cum)).astype(BF16), ubs[sq])
        s_ref[sq] = s_new.reshape(GDN_HEADS, GDN_DK, GDN_DV)
        on = _gated_rmsnorm(os_[sq], ng_ref[...], stack(z_ref[rsl[sq], :]))
        for h in range(GDN_HEADS):
            o_ref[rsl[sq], h * GDN_DK:(h + 1) * GDN_DK] = on[h * c:(h + 1) * c, :]


def _gdn_sample(layer, q, k, v, gb, z, state_all, norm_g):
    bs = state_all.shape[1]
    sq = GDN_SAMPLE_SEQS
    row = lambda i: (i, 0)
    wide = pl.BlockSpec((sq * NEW_PAD, GDN_QK), row)
    st = pl.BlockSpec((sq, GDN_HEADS, GDN_DK, GDN_DV), lambda i: (i, 0, 0, 0))
    return pl.pallas_call(
        _gdn_sample_kernel,
        grid=(bs // sq,),
        in_specs=[wide, wide, wide, pl.BlockSpec((sq * NEW_PAD, LANES), row), wide,
                  pl.BlockSpec((None, sq, GDN_HEADS, GDN_DK, GDN_DV), lambda i: (layer, i, 0, 0, 0)),
                  pl.BlockSpec((1, GDN_DV), lambda i: (0, 0))],
        out_specs=[wide, st],
        out_shape=[jax.ShapeDtypeStruct((bs * NEW_PAD, GDN_QK), BF16),
                   jax.ShapeDtypeStruct((bs, GDN_HEADS, GDN_DK, GDN_DV), F32)],
        compiler_params=_cparams(("parallel",)),
        name="gdn_sample_chunk",
    )(q, k, v, gb, z, state_all, norm_g.reshape(1, -1))


def _merge_kernel(np_tiles, cp_ref, mp_ref, gp_ref, cs_ref, ms_ref, gs_ref, wc_ref, wm_ref, wg_ref,
                  g0_ref, g1_ref, g2_ref, o_ref):
    def merged(c_ref, m_ref, g_ref):
        y = (_sigmoid(g0_ref[...]) * _dot(c_ref[...], wc_ref[...])
             + _sigmoid(g1_ref[...]) * _dot(m_ref[...], wm_ref[...])
             + _sigmoid(g2_ref[...]) * _dot(g_ref[...], wg_ref[...]))
        o_ref[...] = y.astype(BF16)

    @pl.when(pl.program_id(0) < np_tiles)
    def _():
        merged(cp_ref, mp_ref, gp_ref)

    @pl.when(pl.program_id(0) >= np_tiles)
    def _():
        merged(cs_ref, ms_ref, gs_ref)


def _merge(prompt_acts, sample_acts, w_pw, w_mo, w_go, act):
    n_p, n_s = prompt_acts[0].shape[0], sample_acts[0].shape[0]
    tm = _row_tile(int(np.gcd(n_p, n_s)), ROW_TILE)
    np_tiles = n_p // tm
    ap = pl.BlockSpec((tm, CONV_CH), lambda i: (jnp.minimum(i, np_tiles - 1), 0))
    asmp = pl.BlockSpec((tm, CONV_CH), lambda i: (jnp.maximum(i - np_tiles, 0), 0))
    w = pl.BlockSpec((CONV_CH, D_MODEL), lambda i: (0, 0))

    def gate(br):
        return pl.BlockSpec((tm, D_MODEL), lambda i: (i, COL_GATE // D_MODEL + br))

    return pl.pallas_call(
        functools.partial(_merge_kernel, np_tiles),
        grid=((n_p + n_s) // tm,),
        in_specs=[ap, ap, ap, asmp, asmp, asmp, w, w, w, gate(0), gate(1), gate(2)],
        out_specs=pl.BlockSpec((tm, D_MODEL), lambda i: (i, 0)),
        out_shape=jax.ShapeDtypeStruct((n_p + n_s, D_MODEL), BF16),
        compiler_params=_cparams(("parallel",)),
        name="merge_branches",
    )(*prompt_acts, *sample_acts, w_pw, w_mo, w_go, act, act, act)


def _route(x1, w_r, b_r, e_ref, p_ref):
    logits = _dot_x3(_split_bf16(x1), _split_bf16(w_r)) + b_r
    lane = lax.broadcasted_iota(jnp.int32, logits.shape, 1).astype(F32)
    big = jnp.float32(1e9)
    lg = jnp.where(lane < N_GROUPS, logits, -jnp.inf)
    mg = jnp.max(lg, axis=-1, keepdims=True)
    g_val = 1.0 / jnp.sum(jnp.exp(lg - mg), axis=-1, keepdims=True)
    g_idx = jnp.min(jnp.where(lg == mg, lane, big), axis=-1, keepdims=True)
    lo = N_GROUPS + g_idx * EXPERTS_PER_GROUP
    le = jnp.where((lane >= lo) & (lane < lo + EXPERTS_PER_GROUP), logits, -jnp.inf)
    m1 = jnp.max(le, axis=-1, keepdims=True)
    i1 = jnp.min(jnp.where(le == m1, lane, big), axis=-1, keepdims=True)
    le2 = jnp.where(lane == i1, -jnp.inf, le)
    m2 = jnp.max(le2, axis=-1, keepdims=True)
    i2 = jnp.min(jnp.where(le2 == m2, lane, big), axis=-1, keepdims=True)
    se = jnp.sum(jnp.exp(le - m1), axis=-1, keepdims=True)
    p1 = 1.0 / se
    p2 = jnp.exp(m2 - m1) / se
    w1 = g_val * (p1 / (p1 + p2))
    w2 = g_val * (p2 / (p1 + p2))
    e_ref[...] = jnp.where(lane == 0, i1 - N_GROUPS, jnp.where(lane == 1, i2 - N_GROUPS, 0.0)).astype(jnp.int32)
    p_ref[...] = jnp.where(lane == 0, w1, jnp.where(lane == 1, w2, 0.0))


def _out_ln_kernel(alpha, m_ref, w_ref, x_ref, g_ref, b_ref, wr_ref, br_ref, o_ref, e_ref, p_ref):
    y = alpha * x_ref[...] + _dot(m_ref[...], w_ref[...])
    x1 = _ln_rows(y, g_ref[...], b_ref[...])
    o_ref[...] = x1
    _route(x1, wr_ref[...], br_ref[...], e_ref, p_ref)


def _out_ln_route(merged, w_out, xt, ln_g, ln_b, alpha, w_r, b_r):
    n = xt.shape[0]
    tm = ROW_TILE
    vec = pl.BlockSpec((1, D_MODEL), lambda i: (0, 0))
    small = pl.BlockSpec((tm, LANES), lambda i: (i, 0))
    return pl.pallas_call(
        functools.partial(_out_ln_kernel, alpha),
        grid=(n // tm,),
        in_specs=[pl.BlockSpec((tm, D_MODEL), lambda i: (i, 0)),
                  pl.BlockSpec((D_MODEL, D_MODEL), lambda i: (0, 0)),
                  pl.BlockSpec((tm, D_MODEL), lambda i: (i, 0)), vec, vec,
                  pl.BlockSpec((D_MODEL, LANES), lambda i: (0, 0)),
                  pl.BlockSpec((1, LANES), lambda i: (0, 0))],
        out_specs=[pl.BlockSpec((tm, D_MODEL), lambda i: (i, 0)), small, small],
        out_shape=[jax.ShapeDtypeStruct((n, D_MODEL), F32), jax.ShapeDtypeStruct((n, LANES), jnp.int32),
                   jax.ShapeDtypeStruct((n, LANES), F32)],
        compiler_params=_cparams(("parallel",)),
        name="out_proj_ln1_route",
    )(merged, w_out, xt, ln_g.reshape(1, -1), ln_b.reshape(1, -1), w_r, b_r)


def _gather_rows(src_hbm, dst, sem, idx_ref, base, count):
    def body(r, carry):
        pltpu.make_async_copy(src_hbm.at[pl.ds(idx_ref[base + r], 1)], dst.at[pl.ds(r, 1)], sem).start()
        return carry
    lax.fori_loop(0, count, body, 0, unroll=8)


def _wait_rows(src_hbm, dst, sem, count):
    pltpu.make_async_copy(src_hbm.at[pl.ds(0, count)], dst.at[pl.ds(0, count)], sem).wait()


def _moe_kernel(te_ref, src_ref, nu_ref, x_hbm, wg_ref, wu_ref, wd_ref, y_ref,
                xbuf, sem, wgb, wub, wdb):
    t = pl.program_id(0)
    nt = pl.num_programs(0)
    n_used = nu_ref[0]
    slot = t % 2
    te = MOE_TILE

    @pl.when(t == 0)
    def _():
        _gather_rows(x_hbm, xbuf.at[0], sem.at[0], src_ref, 0, te)

    @pl.when(t + 1 < n_used)
    def _():
        _gather_rows(x_hbm, xbuf.at[1 - slot], sem.at[1 - slot], src_ref, (t + 1) * te, te)

    changed = jnp.logical_or(t == 0, te_ref[t] != te_ref[jnp.maximum(t - 1, 0)])

    @pl.when(jnp.logical_and(changed, t < n_used))
    def _():
        wgb[...] = wg_ref[...].astype(BF16)
        wub[...] = wu_ref[...].astype(BF16)
        wdb[...] = wd_ref[...].astype(BF16)

    @pl.when(jnp.logical_or(t < n_used, t == 0))
    def _():
        _wait_rows(x_hbm, xbuf.at[slot], sem.at[slot], te)

    @pl.when(t < n_used)
    def _():
        xb = xbuf[slot].astype(BF16)
        h = _silu(_dot(xb, wgb[...])) * _dot(xb, wub[...])
        y_ref[...] = _dot(h.astype(BF16), wdb[...])

    @pl.when(t >= n_used)
    def _():
        y_ref[...] = jnp.zeros_like(y_ref)


def _moe_experts(layer, tile_expert, row_src, n_used, x1, w_gate, w_up, w_down):
    n_tiles = tile_expert.shape[0]
    te = MOE_TILE
    grid_spec = pltpu.PrefetchScalarGridSpec(
        num_scalar_prefetch=3,
        grid=(n_tiles,),
        in_specs=[pl.BlockSpec(memory_space=pl.ANY),
                  pl.BlockSpec((None, None, D_MODEL, EXPERT_FF), lambda t, te_r, s_r, n_r: (layer, te_r[t], 0, 0)),
                  pl.BlockSpec((None, None, D_MODEL, EXPERT_FF), lambda t, te_r, s_r, n_r: (layer, te_r[t], 0, 0)),
                  pl.BlockSpec((None, None, EXPERT_FF, D_MODEL), lambda t, te_r, s_r, n_r: (layer, te_r[t], 0, 0))],
        out_specs=pl.BlockSpec((te, D_MODEL), lambda t, te_r, s_r, n_r: (t, 0)),
        scratch_shapes=[pltpu.VMEM((2, te, D_MODEL), F32), pltpu.SemaphoreType.DMA((2,)),
                        pltpu.VMEM((D_MODEL, EXPERT_FF), BF16), pltpu.VMEM((D_MODEL, EXPERT_FF), BF16),
                        pltpu.VMEM((EXPERT_FF, D_MODEL), BF16)],
    )
    return pl.pallas_call(
        _moe_kernel,
        grid_spec=grid_spec,
        out_shape=jax.ShapeDtypeStruct((n_tiles * te, D_MODEL), F32),
        compiler_params=_cparams(("arbitrary",)),
        name="moe_experts",
    )(tile_expert, row_src, n_used, x1, w_gate, w_up, w_down)


def _combine_ln_kernel(alpha, pos_ref, y_hbm, x_ref, p_ref, g_ref, b_ref, o_ref, ob_ref, ybuf, sem):
    t = pl.program_id(0)
    nt = pl.num_programs(0)
    tm = x_ref.shape[0]
    slot = t % 2

    @pl.when(t == 0)
    def _():
        _gather_rows(y_hbm, ybuf.at[0], sem.at[0], pos_ref, 0, 2 * tm)

    @pl.when(t + 1 < nt)
    def _():
        _gather_rows(y_hbm, ybuf.at[1 - slot], sem.at[1 - slot], pos_ref, (t + 1) * 2 * tm, 2 * tm)

    _wait_rows(y_hbm, ybuf.at[slot], sem.at[slot], 2 * tm)
    p = p_ref[...]
    y = alpha * x_ref[...] + p[:, 0:1] * ybuf[slot, 0:tm, :] + p[:, 1:2] * ybuf[slot, tm:2 * tm, :]
    x2 = _ln_rows(y, g_ref[...], b_ref[...])
    o_ref[...] = x2
    ob_ref[...] = x2.astype(BF16)


def _combine_ln(pos, y_sorted, x1, wts, ln_g, ln_b, alpha):
    n = x1.shape[0]
    tm = ROW_TILE
    vec = pl.BlockSpec((1, D_MODEL), lambda i, p: (0, 0))
    grid_spec = pltpu.PrefetchScalarGridSpec(
        num_scalar_prefetch=1,
        grid=(n // tm,),
        in_specs=[pl.BlockSpec(memory_space=pl.ANY),
                  pl.BlockSpec((tm, D_MODEL), lambda i, p: (i, 0)),
                  pl.BlockSpec((tm, LANES), lambda i, p: (i, 0)), vec, vec],
        out_specs=[pl.BlockSpec((tm, D_MODEL), lambda i, p: (i, 0)), pl.BlockSpec((tm, D_MODEL), lambda i, p: (i, 0))],
        scratch_shapes=[pltpu.VMEM((2, 2 * tm, D_MODEL), F32), pltpu.SemaphoreType.DMA((2,))],
    )
    return pl.pallas_call(
        functools.partial(_combine_ln_kernel, alpha),
        grid_spec=grid_spec,
        out_shape=[jax.ShapeDtypeStruct((n, D_MODEL), F32), jax.ShapeDtypeStruct((n, D_MODEL), BF16)],
        compiler_params=_cparams(("arbitrary",)),
        name="moe_combine_ln2",
    )(pos, y_sorted, x1, wts, ln_g.reshape(1, -1), ln_b.reshape(1, -1))


def _routing_tables(eid, n):
    te = MOE_TILE
    n_tiles = (2 * n) // te + N_EXPERTS
    e = eid.reshape(-1)
    onehot = (e[:, None] == jnp.arange(N_EXPERTS, dtype=jnp.int32)[None, :]).astype(jnp.int32)
    rank = jnp.sum((jnp.cumsum(onehot, axis=0) - onehot) * onehot, axis=1)
    counts = jnp.sum(onehot, axis=0)
    tiles_per = (counts + te - 1) // te
    tile_end = jnp.cumsum(tiles_per)
    tile_start = tile_end - tiles_per
    dest = tile_start[e] * te + rank
    n_used = tile_end[-1:].astype(jnp.int32)
    tile_expert = jnp.minimum(jnp.searchsorted(tile_end, jnp.arange(n_tiles, dtype=jnp.int32), side="right"),
                              N_EXPERTS - 1).astype(jnp.int32)
    token = jnp.arange(2 * n, dtype=jnp.int32) // 2
    row_src = jnp.zeros((n_tiles * te,), jnp.int32).at[dest].set(token)
    tm = ROW_TILE
    pos = dest.reshape(n // tm, tm, 2).transpose(0, 2, 1).reshape(-1).astype(jnp.int32)
    return tile_expert, row_src, n_used, pos


def _rot_cols(w):
    half = ROPE_DIM // 2
    return jnp.concatenate([-w[..., half:], w[..., :half]], axis=-1)


def _pack_w_in_t(w_in):
    d = w_in.shape[0]
    wt = jnp.swapaxes(w_in, 0, 1)
    offs = np.cumsum([0, 2 * CONV_CH, Q_LORA, KV_LORA, ROPE_DIM, GDN_QKV, GDN_HEADS * GDN_DV, GDN_HEADS, GDN_HEADS,
                      N_BRANCHES * D_MODEL])
    seg = [wt[offs[i]:offs[i + 1]] for i in range(9)]
    glu, cq, ckv, kr, qkv, z, b_raw, a_raw, gate = seg
    half = ROPE_DIM // 2
    kr_rot = jnp.concatenate([-kr[half:], kr[:half]], axis=0)
    ba = jnp.concatenate([b_raw, a_raw, jnp.zeros((LANES - 2 * GDN_HEADS, d), w_in.dtype)], axis=0)
    rows = [gate, qkv, glu[:CONV_CH], glu[CONV_CH:], z, cq, ckv, kr, kr_rot, ba,
            jnp.zeros((N_PACK - COL_BA - LANES, d), w_in.dtype)]
    return jnp.concatenate(rows, axis=0)


def _pack_w_uq(w_uq):
    w = w_uq.reshape(Q_LORA, MLA_HEADS, NOPE_DIM + ROPE_DIM)
    nope, rope = w[..., :NOPE_DIM], w[..., NOPE_DIM:]
    zpad = jnp.zeros((Q_LORA, MLA_HEADS, HEAD_QK - NOPE_DIM - ROPE_DIM), w.dtype)
    wa = jnp.concatenate([nope, rope, zpad], axis=-1)
    wb = jnp.concatenate([jnp.zeros_like(nope), _rot_cols(rope), zpad], axis=-1)
    return wa.reshape(Q_LORA, -1).astype(BF16), wb.reshape(Q_LORA, -1).astype(BF16)


def _rope_tables(pos, scale):
    half = ROPE_DIM // 2
    inv_freq = ROPE_THETA ** (-jnp.arange(half, dtype=F32) / half)
    ang = pos.astype(F32)[:, None] * inv_freq[None, :]
    cos = jnp.concatenate([jnp.cos(ang), jnp.cos(ang)], axis=-1)
    sin = jnp.concatenate([jnp.sin(ang), jnp.sin(ang)], axis=-1)
    n = pos.shape[0]
    zpad = jnp.zeros((n, HEAD_QK - NOPE_DIM - ROPE_DIM), F32)
    tab_a = jnp.concatenate([jnp.full((n, NOPE_DIM), scale, F32), scale * cos, zpad], axis=-1)
    tab_b = jnp.concatenate([jnp.zeros((n, NOPE_DIM), F32), scale * sin, zpad], axis=-1)
    tab_cs = jnp.concatenate([cos, sin], axis=-1)
    return tab_a, tab_b, tab_cs


def kernel(x_prompt, x_sample, cache_ckv, cache_krope, state_conf_conv, state_gdn_conv, state_gdn, page_table, w_in, conv_dw, conv_dw_b, conv_ln_g, conv_ln_b, conv_pw, mla_q_norm, mla_w_uq, mla_kv_norm, mla_w_uk, mla_w_uv, mla_w_o, gdn_conv_w, gdn_a_log, gdn_dt_bias, gdn_norm, gdn_w_o, w_out, ln1_g, ln1_b, router_group_w, router_group_b, router_expert_w, router_expert_b, moe_w_gate, moe_w_up, moe_w_down, ln2_g, ln2_b):
    bp, tp, d = x_prompt.shape
    bs, ts, _ = x_sample.shape
    depth = w_in.shape[0]
    n_pages, page = page_table.shape[1], cache_ckv.shape[2]
    past_len = n_pages * page
    n_p, n_s = bp * tp, bs * ts
    n = n_p + n_s
    assert d == D_MODEL and w_in.shape[2] == 2 * CONV_CH + Q_LORA + KV_LORA + ROPE_DIM + GDN_QKV + GDN_HEADS * GDN_DV \
        + 2 * GDN_HEADS + N_BRANCHES * D_MODEL
    assert tp % ATT_TILE == 0 and tp % GDN_CHUNK == 0 and n % ROW_TILE == 0 and n_p % ROW_TILE == 0
    assert ts <= NEW_PAD and bs % SAMPLE_BB == 0 and n_p % (SAMPLE_BB * ts) == 0 and n_pages % PAGES_PER_STEP == 0
    assert (2 * n) % MOE_TILE == 0 and bs % GDN_SAMPLE_SEQS == 0 and tp % (GDN_CHUNKS_PER_STEP * GDN_CHUNK) == 0
    alpha = float((2.0 * depth) ** 0.25)
    scale = float((NOPE_DIM + ROPE_DIM) ** -0.5)

    pos = jnp.concatenate([jnp.tile(jnp.arange(tp, dtype=jnp.int32), bp),
                           jnp.tile(past_len + jnp.arange(ts, dtype=jnp.int32), bs)])
    tab_a, tab_b, tab_cs = _rope_tables(pos, scale)
    e_place = jnp.zeros((LANES, MLA_HEADS, HEAD_QK), F32)
    e_place = e_place.at[jnp.arange(ROPE_DIM)[:, None], jnp.arange(MLA_HEADS)[None, :],
                         NOPE_DIM + jnp.arange(ROPE_DIM)[:, None]].set(1.0).reshape(LANES, -1).astype(BF16)
    zeros_state = jnp.zeros((bp, GDN_HEADS, GDN_DK, GDN_DV), F32)
    cache_krope_t = jnp.swapaxes(cache_krope, 2, 3)

    xt = jnp.concatenate([x_prompt.reshape(n_p, d), x_sample.reshape(n_s, d)], axis=0)
    xb = xt.astype(BF16)
    outs = [[] for _ in range(10)]
    for l in range(depth):
        w_pack = _pack_w_in_t(w_in[l])
        w_qa, w_qb = _pack_w_uq(mla_w_uq[l])
        w_uk = mla_w_uk[l]
        w_uk_pad = jnp.concatenate([w_uk, jnp.zeros((KV_LORA, MLA_HEADS, HEAD_QK - NOPE_DIM), F32)],
                                   axis=-1).reshape(KV_LORA, -1).astype(BF16)
        w_uk_t = jnp.transpose(w_uk, (1, 2, 0)).astype(BF16)
        w_uv = mla_w_uv[l].reshape(KV_LORA, -1).astype(BF16)
        al = jnp.zeros((2, LANES), F32).at[0, GDN_HEADS:2 * GDN_HEADS].set(gdn_a_log[l]) \
            .at[1, GDN_HEADS:2 * GDN_HEADS].set(gdn_dt_bias[l])
        w_r = jnp.concatenate([router_group_w[l], router_expert_w[l],
                               jnp.zeros((d, LANES - N_GROUPS - N_EXPERTS), F32)], axis=1)
        b_r = jnp.concatenate([router_group_b[l], router_expert_b[l],
                               jnp.zeros((LANES - N_GROUPS - N_EXPERTS,), F32)]).reshape(1, LANES)

        act = _inproj(xb, w_pack)

        c_p, u_p = _conf_prompt(act, bp, tp, conv_dw[l], conv_dw_b[l], conv_ln_g[l], conv_ln_b[l])
        c_s, conf_buf_s = _conf_sample(act, n_p, bs, ts, state_conf_conv[l], conv_dw[l], conv_dw_b[l],
                                       conv_ln_g[l], conv_ln_b[l])
        conf_buf_p = u_p.reshape(bp, tp, CONV_CH)[:, tp - (CONV_WIDTH - 1):]

        q = _mla_q(act, mla_q_norm[l], w_qa, w_qb, tab_a, tab_b)
        ckv, krope, k_full, v_full = _mla_kv(act, mla_kv_norm[l], tab_cs, w_uk_pad, w_uv, e_place)
        o_p = _flash_prompt(q, k_full, v_full, bp, tp)
        q_abs = _q_absorb(q[n_p:], w_uk_t).reshape(bs, ts * MLA_HEADS, Q_ABS)
        ckv_s = ckv[n_p:].reshape(bs, ts, KV_LORA)
        kr_s = krope[n_p:].reshape(bs, ts, ROPE_DIM)
        pad = ((0, 0), (0, NEW_PAD - ts), (0, 0))
        kr_new_t = jnp.pad(jnp.swapaxes(kr_s, 1, 2), ((0, 0), (0, 0), (0, LANES - ts)))
        o_lat = _paged_attention(l, page_table, q_abs, jnp.pad(ckv_s, pad), kr_new_t,
                                 cache_ckv, cache_krope_t, ts)
        o_s = _value_up(o_lat.reshape(n_s, MLA_HEADS * KV_LORA), w_uv)

        qg_p, kg_p, vg_p, gb_p = _gdn_prep_prompt(act, bp, tp, gdn_conv_w[l], al)
        og_p, st_p = _gdn_scan(*_gdn_intra(qg_p, kg_p, vg_p, gb_p), act, zeros_state, gdn_norm[l],
                               bp, tp // GDN_CHUNK)
        qg_s, kg_s, vg_s, gb_s = _gdn_prep_sample(act, n_p, bs, ts, state_gdn_conv[l], gdn_conv_w[l], al)
        z_s = jnp.pad(act[n_p:, COL_Z:COL_Z + GDN_QK].reshape(bs, ts, GDN_QK), pad).reshape(bs * NEW_PAD, GDN_QK)
        og_s, st_s = _gdn_sample(l, qg_s, kg_s, vg_s, gb_s, z_s, state_gdn, gdn_norm[l])
        og_s = og_s.reshape(bs, NEW_PAD, GDN_QK)[:, :ts].reshape(n_s, GDN_QK)
        nb = SHORT_CONV - 1
        gconv_p = act[:n_p].reshape(bp, tp, N_PACK)[:, tp - nb:, COL_QKV:COL_QKV + GDN_QKV]
        gconv_s = act[n_p:].reshape(bs, ts, N_PACK)[:, max(ts - nb, 0):, COL_QKV:COL_QKV + GDN_QKV]
        if ts < nb:
            gconv_s = jnp.concatenate([state_gdn_conv[l][:, ts:], gconv_s], axis=1)

        merged = _merge((c_p, o_p, og_p), (c_s, o_s, og_s), conv_pw[l].astype(BF16), mla_w_o[l].astype(BF16),
                        gdn_w_o[l].astype(BF16), act)
        x1, eid, wts = _out_ln_route(merged, w_out[l].astype(BF16), xt, ln1_g[l], ln1_b[l], alpha, w_r, b_r)

        tile_expert, row_src, n_used, pos_tab = _routing_tables(eid[:, :2], n)
        y_sorted = _moe_experts(l, tile_expert, row_src, n_used, x1, moe_w_gate, moe_w_up, moe_w_down)
        xt, xb = _combine_ln(pos_tab, y_sorted, x1, wts, ln2_g[l], ln2_b[l], alpha)

        new = [ckv[:n_p].reshape(bp, tp, KV_LORA), krope[:n_p].reshape(bp, tp, ROPE_DIM), ckv_s, kr_s,
               conf_buf_p, conf_buf_s, gconv_p, gconv_s, st_p, st_s]
        for lst, val in zip(outs, new):
            lst.append(val)

    return (xt[:n_p].reshape(bp, tp, d), xt[n_p:].reshape(bs, ts, d)) + tuple(jnp.stack(o) for o in outs)
```

```python
import functools

import numpy as np
import jax
import jax.numpy as jnp
from jax import lax
from jax.experimental import pallas as pl
from jax.experimental.pallas import tpu as pltpu

F32 = jnp.float32
BF16 = jnp.bfloat16
HIGHEST = lax.Precision.HIGHEST

D_MODEL = 2048
CONV_CH = 1024
CONV_WIDTH = 31
MLA_HEADS = 8
Q_LORA = 512
KV_LORA = 512
NOPE_DIM = 128
ROPE_DIM = 64
V_DIM = 128
ROPE_THETA = 10000.0
GDN_HEADS = 8
GDN_DK = 128
GDN_DV = 128
GDN_QK = GDN_HEADS * GDN_DK
GDN_QKV = 2 * GDN_QK + GDN_HEADS * GDN_DV
SHORT_CONV = 4
GDN_CHUNK = 64
N_GROUPS = 4
EXPERTS_PER_GROUP = 8
N_EXPERTS = N_GROUPS * EXPERTS_PER_GROUP
EXPERT_FF = 512
N_BRANCHES = 3
EPS = 1e-6

LANES = 128
SUBLANES = 8
VMEM_LIMIT_BYTES = 56 * 1024 * 1024

COL_GATE = 0
COL_QKV = 6144
COL_A = 9216
COL_G = 10240
COL_Z = 11264
COL_CQ = 12288
COL_CKV = 12800
COL_KR = 13312
COL_BA = 13440
N_PACK = 13824
TN_IN = 512

HEAD_QK = 256
Q_ABS = 640
MOE_TILE = 256
MOE_SLOTS = 3
ROW_TILE = 256


def _cparams(sem):
    return pltpu.CompilerParams(dimension_semantics=sem, vmem_limit_bytes=VMEM_LIMIT_BYTES)


def _dot(a, b, precision=None):
    return jnp.dot(a, b, preferred_element_type=F32, precision=precision)


def _dot_nt(a, b, precision=None):
    return lax.dot_general(a, b, (((1,), (1,)), ((), ())), preferred_element_type=F32, precision=precision)


def _dot_tn(a, b, precision=None):
    return lax.dot_general(a, b, (((0,), (0,)), ((), ())), preferred_element_type=F32, precision=precision)


def _sigmoid(x):
    return 1.0 / (1.0 + jnp.exp(-x))


def _silu(x):
    return x * _sigmoid(x)


def _row_tile(n, cap):
    best = SUBLANES
    for t in range(SUBLANES, cap + 1, SUBLANES):
        if n % t == 0:
            best = t
    return best


def _inproj_kernel(x_ref, w_ref, o_ref):
    o_ref[...] = _dot_nt(x_ref[...], w_ref[...].astype(BF16))


INPROJ_ROWS = 2200


def _inproj(xb, w_pack_t):
    n, d = xb.shape
    tm = _row_tile(n, INPROJ_ROWS)
    return pl.pallas_call(
        _inproj_kernel,
        grid=(n // tm, N_PACK // TN_IN),
        in_specs=[pl.BlockSpec((tm, d), lambda i, j: (i, 0)),
                  pl.BlockSpec((TN_IN, d), lambda i, j: (j, 0))],
        out_specs=pl.BlockSpec((tm, TN_IN), lambda i, j: (i, j)),
        out_shape=jax.ShapeDtypeStruct((n, N_PACK), F32),
        compiler_params=_cparams(("parallel", "arbitrary")),
        name="inproj",
    )(xb, w_pack_t)


def _ln_rows(y, g, b):
    mu = jnp.mean(y, axis=-1, keepdims=True)
    yc = y - mu
    var = jnp.mean(yc * yc, axis=-1, keepdims=True)
    return yc * lax.rsqrt(var + EPS) * g + b


CONV_HALO = 32


def _conf_prompt_kernel(a_ref, g_ref, ap_ref, gp_ref, w_ref, b_ref, lg_ref, lb_ref, act_ref, u_ref, xp_ref, y_ref):
    tt = a_ref.shape[0]
    u = a_ref[...] * _sigmoid(g_ref[...])
    u_ref[...] = u
    xp_ref[CONV_HALO:, :] = u
    up = ap_ref[...] * _sigmoid(gp_ref[...])
    xp_ref[0:CONV_HALO, :] = jnp.where(pl.program_id(1) > 0, up, 0.0)
    off = CONV_HALO - (CONV_WIDTH - 1)
    for c in range(CONV_CH // LANES):
        cs = slice(c * LANES, (c + 1) * LANES)
        acc = jnp.zeros((tt, LANES), F32)
        for j in range(CONV_WIDTH):
            acc = acc + xp_ref[off + j:off + j + tt, cs] * w_ref[j:j + 1, cs]
        y_ref[:, cs] = acc + b_ref[:, cs]
    yn = _ln_rows(y_ref[...], lg_ref[...], lb_ref[...])
    act_ref[...] = _silu(yn).astype(BF16)


def _conf_prompt(act, bp, tp, conv_dw, dw_b, ln_g, ln_b):
    tt = ROW_TILE
    nt = tp // tt
    ca, cg = COL_A // CONV_CH, COL_G // CONV_CH
    r = tt // CONV_HALO

    def cur(col):
        return pl.BlockSpec((tt, CONV_CH), lambda b, i: (b * nt + i, col))

    def prev(col):
        return pl.BlockSpec((CONV_HALO, CONV_CH), lambda b, i: (jnp.maximum((b * nt + i) * r - 1, 0), col))

    vec = pl.BlockSpec((1, CONV_CH), lambda b, i: (0, 0))
    return pl.pallas_call(
        _conf_prompt_kernel,
        grid=(bp, nt),
        in_specs=[cur(ca), cur(cg), prev(ca), prev(cg),
                  pl.BlockSpec((CONV_WIDTH, CONV_CH), lambda b, i: (0, 0)), vec, vec, vec],
        out_specs=[pl.BlockSpec((tt, CONV_CH), lambda b, i: (b * nt + i, 0)),
                   pl.BlockSpec((tt, CONV_CH), lambda b, i: (b * nt + i, 0))],
        out_shape=[jax.ShapeDtypeStruct((bp * tp, CONV_CH), BF16),
                   jax.ShapeDtypeStruct((bp * tp, CONV_CH), F32)],
        scratch_shapes=[pltpu.VMEM((tt + CONV_HALO, CONV_CH), F32), pltpu.VMEM((tt, CONV_CH), F32)],
        compiler_params=_cparams(("parallel", "arbitrary")),
        name="conf_conv_prompt",
    )(act, act, act, act, conv_dw, dw_b.reshape(1, -1), ln_g.reshape(1, -1), ln_b.reshape(1, -1))


SAMPLE_BB = 8


def _conf_sample_kernel(ts, a_ref, g_ref, buf_ref, w_ref, b_ref, lg_ref, lb_ref, act_ref, nbuf_ref, xp_ref, y_ref):
    nb = CONV_WIDTH - 1
    u = a_ref[...] * _sigmoid(g_ref[...])
    w = w_ref[...]
    for bi in range(SAMPLE_BB):
        xp_ref[0:nb, :] = buf_ref[bi]
        xp_ref[nb:nb + ts, :] = u[bi * ts:(bi + 1) * ts, :]
        nbuf_ref[bi] = xp_ref[ts:ts + nb, :]
        for t in range(ts):
            y_ref[bi * ts + t:bi * ts + t + 1, :] = jnp.sum(xp_ref[t:t + CONV_WIDTH, :] * w, axis=0, keepdims=True)
    yn = _ln_rows(y_ref[...] + b_ref[...], lg_ref[...], lb_ref[...])
    act_ref[...] = _silu(yn).astype(BF16)


def _conf_sample(act, n_p, bs, ts, buf, conv_dw, dw_b, ln_g, ln_b):
    bb = SAMPLE_BB
    rows = bb * ts
    nb = CONV_WIDTH - 1
    base = n_p // rows
    vec = pl.BlockSpec((1, CONV_CH), lambda i: (0, 0))
    return pl.pallas_call(
        functools.partial(_conf_sample_kernel, ts),
        grid=(bs // bb,),
        in_specs=[pl.BlockSpec((rows, CONV_CH), lambda i: (base + i, COL_A // CONV_CH)),
                  pl.BlockSpec((rows, CONV_CH), lambda i: (base + i, COL_G // CONV_CH)),
                  pl.BlockSpec((bb, nb, CONV_CH), lambda i: (i, 0, 0)),
                  pl.BlockSpec((CONV_WIDTH, CONV_CH), lambda i: (0, 0)), vec, vec, vec],
        out_specs=[pl.BlockSpec((rows, CONV_CH), lambda i: (i, 0)),
                   pl.BlockSpec((bb, nb, CONV_CH), lambda i: (i, 0, 0))],
        out_shape=[jax.ShapeDtypeStruct((bs * ts, CONV_CH), BF16),
                   jax.ShapeDtypeStruct((bs, nb, CONV_CH), F32)],
        scratch_shapes=[pltpu.VMEM((nb + ts + 6, CONV_CH), F32), pltpu.VMEM((rows, CONV_CH), F32)],
        compiler_params=_cparams(("parallel",)),
        name="conf_conv_sample",
    )(act, act, buf, conv_dw, dw_b.reshape(1, -1), ln_g.reshape(1, -1), ln_b.reshape(1, -1))


def _rms_rows(x, g):
    return x * lax.rsqrt(jnp.mean(x * x, axis=-1, keepdims=True) + EPS) * g


def _mla_q_kernel(cq_ref, g_ref, wa_ref, wb_ref, ca_ref, cb_ref, q_ref):
    cqn = _rms_rows(cq_ref[...], g_ref[...]).astype(BF16)
    qa = _dot(cqn, wa_ref[...])
    qb = _dot(cqn, wb_ref[...])
    ca = ca_ref[...]
    cb = cb_ref[...]
    for h in range(MLA_HEADS):
        hs = slice(h * HEAD_QK, (h + 1) * HEAD_QK)
        q_ref[:, hs] = (qa[:, hs] * ca + qb[:, hs] * cb).astype(BF16)


def _mla_q(act, q_norm, w_qa, w_qb, tab_a, tab_b):
    n = act.shape[0]
    tm = ROW_TILE
    nq = MLA_HEADS * HEAD_QK
    return pl.pallas_call(
        _mla_q_kernel,
        grid=(n // tm,),
        in_specs=[pl.BlockSpec((tm, Q_LORA), lambda i: (i, COL_CQ // Q_LORA)),
                  pl.BlockSpec((1, Q_LORA), lambda i: (0, 0)),
                  pl.BlockSpec((Q_LORA, nq), lambda i: (0, 0)),
                  pl.BlockSpec((Q_LORA, nq), lambda i: (0, 0)),
                  pl.BlockSpec((tm, HEAD_QK), lambda i: (i, 0)),
                  pl.BlockSpec((tm, HEAD_QK), lambda i: (i, 0))],
        out_specs=pl.BlockSpec((tm, nq), lambda i: (i, 0)),
        out_shape=jax.ShapeDtypeStruct((n, nq), BF16),
        compiler_params=_cparams(("parallel",)),
        name="mla_q",
    )(act, q_norm.reshape(1, -1), w_qa, w_qb, tab_a, tab_b)


def _mla_kv_kernel(ckv_ref, krr_ref, g_ref, cs_ref, wuk_ref, wuv_ref, e_ref, ckv_out, kr_out, k_out, v_out):
    ckvn = _rms_rows(ckv_ref[...], g_ref[...])
    ckv_out[...] = ckvn
    t = krr_ref[...] * cs_ref[...]
    kr2 = t + pltpu.roll(t, ROPE_DIM, axis=1)
    kr_out[...] = kr2[:, :ROPE_DIM]
    cb = ckvn.astype(BF16)
    k_out[...] = (_dot(cb, wuk_ref[...]) + _dot(kr2.astype(BF16), e_ref[...])).astype(BF16)
    v_out[...] = _dot(cb, wuv_ref[...]).astype(BF16)


def _mla_kv(act, kv_norm, tab_cs, w_uk_pad, w_uv, e_place):
    n = act.shape[0]
    tm = ROW_TILE
    nk = MLA_HEADS * HEAD_QK
    nv = MLA_HEADS * V_DIM
    return pl.pallas_call(
        _mla_kv_kernel,
        grid=(n // tm,),
        in_specs=[pl.BlockSpec((tm, KV_LORA), lambda i: (i, COL_CKV // KV_LORA)),
                  pl.BlockSpec((tm, LANES), lambda i: (i, COL_KR // LANES)),
                  pl.BlockSpec((1, KV_LORA), lambda i: (0, 0)),
                  pl.BlockSpec((tm, LANES), lambda i: (i, 0)),
                  pl.BlockSpec((KV_LORA, nk), lambda i: (0, 0)),
                  pl.BlockSpec((KV_LORA, nv), lambda i: (0, 0)),
                  pl.BlockSpec((LANES, nk), lambda i: (0, 0))],
        out_specs=[pl.BlockSpec((tm, KV_LORA), lambda i: (i, 0)),
                   pl.BlockSpec((tm, ROPE_DIM), lambda i: (i, 0)),
                   pl.BlockSpec((tm, nk), lambda i: (i, 0)),
                   pl.BlockSpec((tm, nv), lambda i: (i, 0))],
        out_shape=[jax.ShapeDtypeStruct((n, KV_LORA), F32),
                   jax.ShapeDtypeStruct((n, ROPE_DIM), F32),
                   jax.ShapeDtypeStruct((n, nk), BF16),
                   jax.ShapeDtypeStruct((n, nv), BF16)],
        compiler_params=_cparams(("parallel",)),
        name="mla_kv",
    )(act, act, kv_norm.reshape(1, -1), tab_cs, w_uk_pad, w_uv, e_place)


ATT_TILE = 512


ATT_SUB = 128


def _flash_kernel(qi_ref, ki_ref, q_ref, k_ref, v_ref, o_ref, m_ref, l_ref, acc_ref):
    pair = pl.program_id(2)
    qi = qi_ref[pair]
    ki = ki_ref[pair]
    t = q_ref.shape[0]
    rep = t // V_DIM

    @pl.when(ki == 0)
    def _():
        m_ref[...] = jnp.full_like(m_ref, -jnp.inf)
        l_ref[...] = jnp.zeros_like(l_ref)
        acc_ref[...] = jnp.zeros_like(acc_ref)

    def update(masked):
        k = k_ref[...]
        v = v_ref[...]
        rsl = [slice(sb * ATT_SUB, (sb + 1) * ATT_SUB) for sb in range(t // ATT_SUB)]
        scores = [_dot_nt(q_ref[rs, :], k) for rs in rsl]
        probs = []
        for sb, (rs, s) in enumerate(zip(rsl, scores)):
            if masked:
                rows = lax.broadcasted_iota(jnp.int32, s.shape, 0) + sb * ATT_SUB
                cols = lax.broadcasted_iota(jnp.int32, s.shape, 1)
                s = jnp.where(cols <= rows, s, -jnp.inf)
            m_old = m_ref[rs, :]
            m_new = jnp.maximum(m_old, jnp.broadcast_to(jnp.max(s, axis=-1, keepdims=True), m_old.shape))
            p = jnp.exp(s - jnp.concatenate([m_new] * rep, axis=1))
            alpha = jnp.exp(m_old - m_new)
            l_ref[rs, :] = alpha * l_ref[rs, :] + jnp.broadcast_to(jnp.sum(p, axis=-1, keepdims=True), m_old.shape)
            acc_ref[rs, :] = alpha * acc_ref[rs, :]
            m_ref[rs, :] = m_new
            probs.append(p.astype(BF16))
        for rs, p in zip(rsl, probs):
            acc_ref[rs, :] += _dot(p, v)

    @pl.when(ki < qi)
    def _():
        update(False)

    @pl.when(ki == qi)
    def _():
        update(True)
        o_ref[...] = (acc_ref[...] / l_ref[...]).astype(BF16)


def _flash_prompt(q, k, v, bp, tp):
    t = ATT_TILE
    nt = tp // t
    pairs = [(a, b) for a in range(nt) for b in range(a + 1)]
    qi_list = jnp.asarray([a for a, _ in pairs], jnp.int32)
    ki_list = jnp.asarray([b for _, b in pairs], jnp.int32)
    grid_spec = pltpu.PrefetchScalarGridSpec(
        num_scalar_prefetch=2,
        grid=(bp, MLA_HEADS, len(pairs)),
        in_specs=[pl.BlockSpec((t, HEAD_QK), lambda b, h, p, qi, ki: (b * nt + qi[p], h)),
                  pl.BlockSpec((t, HEAD_QK), lambda b, h, p, qi, ki: (b * nt + ki[p], h)),
                  pl.BlockSpec((t, V_DIM), lambda b, h, p, qi, ki: (b * nt + ki[p], h))],
        out_specs=pl.BlockSpec((t, V_DIM), lambda b, h, p, qi, ki: (b * nt + qi[p], h)),
        scratch_shapes=[pltpu.VMEM((t, V_DIM), F32), pltpu.VMEM((t, V_DIM), F32), pltpu.VMEM((t, V_DIM), F32)],
    )
    return pl.pallas_call(
        _flash_kernel,
        grid_spec=grid_spec,
        out_shape=jax.ShapeDtypeStruct((bp * tp, MLA_HEADS * V_DIM), BF16),
        compiler_params=_cparams(("parallel", "parallel", "arbitrary")),
        name="mla_flash_prompt",
    )(qi_list, ki_list, q, k, v)


def _q_absorb_kernel(q_ref, wt_ref, o_ref):
    q = q_ref[...]
    o_ref[:, :KV_LORA] = _dot(q[:, :NOPE_DIM], wt_ref[...]).astype(BF16)
    o_ref[:, KV_LORA:] = q[:, NOPE_DIM:]


def _q_absorb(q_s, w_uk_t):
    n_s = q_s.shape[0]
    return pl.pallas_call(
        _q_absorb_kernel,
        grid=(MLA_HEADS,),
        in_specs=[pl.BlockSpec((n_s, HEAD_QK), lambda h: (0, h)),
                  pl.BlockSpec((None, NOPE_DIM, KV_LORA), lambda h: (h, 0, 0))],
        out_specs=pl.BlockSpec((n_s, Q_ABS), lambda h: (0, h)),
        out_shape=jax.ShapeDtypeStruct((n_s, MLA_HEADS * Q_ABS), BF16),
        compiler_params=_cparams(("parallel",)),
        name="mla_q_absorb",
    )(q_s, w_uk_t)


PAGES_PER_STEP = 16
PAGES_PER_GROUP = 4
PAGE_SLOTS = 3
NEW_PAD = 8


def _paged_kernel(layer, ppb, page, ts, pt_ref, q_ref, cn_ref, kn_ref, ck_hbm, kr_hbm, o_ref,
                  ckbuf, krbuf, ck_sem, kr_sem, kc_s, kr_s, nc_s, m_ref, l_ref, acc_ref):
    b_idx = pl.program_id(0)
    s_idx = pl.program_id(1)
    n_steps = pl.num_programs(1)
    g = b_idx * n_steps + s_idx
    total = pl.num_programs(0) * n_steps
    slot = g % PAGE_SLOTS
    gk = PAGES_PER_GROUP * page

    def fetch(gi, dst_slot):
        bb = gi // n_steps
        first = (gi % n_steps) * ppb
        for k in range(ppb):
            pid = pt_ref[bb, first + k]
            pltpu.make_async_copy(ck_hbm.at[layer, pid], ckbuf.at[dst_slot, k], ck_sem.at[dst_slot]).start()
            pltpu.make_async_copy(kr_hbm.at[layer, pid], krbuf.at[dst_slot, k], kr_sem.at[dst_slot]).start()

    ahead = PAGE_SLOTS - 1

    @pl.when(g == 0)
    def _():
        for a in range(ahead):
            @pl.when(a < total)
            def _():
                fetch(a, a)

    @pl.when(g + ahead < total)
    def _():
        fetch(g + ahead, (g + ahead) % PAGE_SLOTS)

    pltpu.make_async_copy(ck_hbm.at[layer, pl.ds(0, ppb)], ckbuf.at[slot], ck_sem.at[slot]).wait()
    pltpu.make_async_copy(kr_hbm.at[layer, pl.ds(0, ppb)], krbuf.at[slot], kr_sem.at[slot]).wait()

    @pl.when(s_idx == 0)
    def _():
        m_ref[...] = jnp.full_like(m_ref, -jnp.inf)
        l_ref[...] = jnp.zeros_like(l_ref)
        acc_ref[...] = jnp.zeros_like(acc_ref)

    for k in range(ppb):
        kc_s[k * page:(k + 1) * page, :] = ckbuf[slot, k].astype(BF16)
        kr_s[:, k * page:(k + 1) * page] = krbuf[slot, k].astype(BF16)
    q = q_ref[0]
    q_lat = q[:, :KV_LORA]
    q_rope = q[:, KV_LORA:KV_LORA + ROPE_DIM]

    def update(scores, vals):
        m_old = m_ref[...]
        m_new = m_old
        for s in scores:
            m_new = jnp.maximum(m_new, jnp.max(s, axis=-1, keepdims=True))
        alpha = jnp.exp(m_old - m_new)
        l_new = alpha * l_ref[...]
        acc = alpha * acc_ref[...]
        for s, val in zip(scores, vals):
            p = jnp.exp(s - m_new)
            l_new = l_new + jnp.sum(p, axis=-1, keepdims=True)
            acc = acc + _dot(p.astype(BF16), val)
        l_ref[...] = l_new
        acc_ref[...] = acc
        m_ref[...] = m_new

    groups = [slice(g * gk, (g + 1) * gk) for g in range(ppb // PAGES_PER_GROUP)]
    update([_dot_nt(q_lat, kc_s[g, :]) + _dot(q_rope, kr_s[:, g]) for g in groups], [kc_s[g, :] for g in groups])

    @pl.when(s_idx == pl.num_programs(1) - 1)
    def _():
        nc_s[...] = jnp.zeros_like(nc_s)
        nc_s[0:NEW_PAD, :] = cn_ref[0].astype(BF16)
        s = _dot_nt(q_lat, nc_s[...]) + _dot(q_rope, kn_ref[0].astype(BF16))
        tok = lax.broadcasted_iota(jnp.int32, s.shape, 0) // MLA_HEADS
        col = lax.broadcasted_iota(jnp.int32, s.shape, 1)
        s = jnp.where((col <= tok) & (col < ts), s, -jnp.inf)
        update([s], [nc_s[...]])
        o_ref[0] = (acc_ref[...] / l_ref[...]).astype(BF16)


def _paged_attention(layer, page_table, q_abs, ckv_new, kr_new_t, cache_ckv, cache_krope_t, ts):
    bs, n_pages = page_table.shape
    page = cache_ckv.shape[2]
    ppb = PAGES_PER_STEP
    rows = ts * MLA_HEADS

    grid_spec = pltpu.PrefetchScalarGridSpec(
        num_scalar_prefetch=1,
        grid=(bs, n_pages // ppb),
        in_specs=[pl.BlockSpec((1, rows, Q_ABS), lambda b, s, pt: (b, 0, 0)),
                  pl.BlockSpec((1, NEW_PAD, KV_LORA), lambda b, s, pt: (b, 0, 0)),
                  pl.BlockSpec((1, ROPE_DIM, LANES), lambda b, s, pt: (b, 0, 0)),
                  pl.BlockSpec(memory_space=pl.ANY), pl.BlockSpec(memory_space=pl.ANY)],
        out_specs=pl.BlockSpec((1, rows, KV_LORA), lambda b, s, pt: (b, 0, 0)),
        scratch_shapes=[pltpu.VMEM((PAGE_SLOTS, ppb, page, KV_LORA), F32),
                        pltpu.VMEM((PAGE_SLOTS, ppb, ROPE_DIM, page), F32),
                        pltpu.SemaphoreType.DMA((PAGE_SLOTS,)), pltpu.SemaphoreType.DMA((PAGE_SLOTS,)),
                        pltpu.VMEM((ppb * page, KV_LORA), BF16), pltpu.VMEM((ROPE_DIM, ppb * page), BF16),
                        pltpu.VMEM((LANES, KV_LORA), BF16),
                        pltpu.VMEM((rows, 1), F32), pltpu.VMEM((rows, 1), F32), pltpu.VMEM((rows, KV_LORA), F32)],
    )
    return pl.pallas_call(
        functools.partial(_paged_kernel, layer, ppb, page, ts),
        grid_spec=grid_spec,
        out_shape=jax.ShapeDtypeStruct((bs, rows, KV_LORA), BF16),
        compiler_params=_cparams(("arbitrary", "arbitrary")),
        name="mla_paged_sample",
    )(page_table, q_abs, ckv_new, kr_new_t, cache_ckv, cache_krope_t)


def _uv_kernel(o_ref, w_ref, out_ref):
    out_ref[...] = _dot(o_ref[...], w_ref[...]).astype(BF16)


def _value_up(o_lat, w_uv):
    n_s = o_lat.shape[0]
    return pl.pallas_call(
        _uv_kernel,
        grid=(MLA_HEADS,),
        in_specs=[pl.BlockSpec((n_s, KV_LORA), lambda h: (0, h)),
                  pl.BlockSpec((KV_LORA, V_DIM), lambda h: (0, h))],
        out_specs=pl.BlockSpec((n_s, V_DIM), lambda h: (0, h)),
        out_shape=jax.ShapeDtypeStruct((n_s, MLA_HEADS * V_DIM), BF16),
        compiler_params=_cparams(("parallel",)),
        name="mla_value_up",
    )(o_lat, w_uv)


def _gdn_post(y, ba, al, q_ref, k_ref, v_ref, gb_ref):
    y = _silu(y)
    for h in range(GDN_HEADS):
        hs = slice(h * GDN_DK, (h + 1) * GDN_DK)
        qh = y[:, hs]
        kh = y[:, GDN_QK + h * GDN_DK:GDN_QK + (h + 1) * GDN_DK]
        q_ref[:, hs] = qh * lax.rsqrt(jnp.sum(qh * qh, axis=-1, keepdims=True) + EPS) * (GDN_DK ** -0.5)
        k_ref[:, hs] = kh * lax.rsqrt(jnp.sum(kh * kh, axis=-1, keepdims=True) + EPS)
    v_ref[...] = y[:, 2 * GDN_QK:]
    xa = ba + al[1:2, :]
    softplus = jnp.maximum(xa, 0.0) + jnp.log(1.0 + jnp.exp(-jnp.abs(xa)))
    lane = lax.broadcasted_iota(jnp.int32, ba.shape, 1)
    gb_ref[...] = jnp.where(lane < GDN_HEADS, _sigmoid(ba), -jnp.exp(al[0:1, :]) * softplus)


def _gdn_prep_prompt_kernel(x_ref, xp_ref, w_ref, ba_ref, al_ref, q_ref, k_ref, v_ref, gb_ref, s_ref):
    tt = x_ref.shape[0]
    s_ref[SUBLANES:, :] = x_ref[...]
    s_ref[0:SUBLANES, :] = jnp.where(pl.program_id(1) > 0, xp_ref[...], 0.0)
    off = SUBLANES - (SHORT_CONV - 1)
    y = jnp.zeros((tt, GDN_QKV), F32)
    for j in range(SHORT_CONV):
        y = y + s_ref[off + j:off + j + tt, :] * w_ref[j:j + 1, :]
    _gdn_post(y, ba_ref[...], al_ref[...], q_ref, k_ref, v_ref, gb_ref)


def _gdn_prep_prompt(act, bp, tp, conv_w, al):
    tt = 128
    nt = tp // tt
    r = tt // SUBLANES
    cq = COL_QKV // GDN_QKV
    row = lambda b, i: (b * nt + i, 0)
    return pl.pallas_call(
        _gdn_prep_prompt_kernel,
        grid=(bp, nt),
        in_specs=[pl.BlockSpec((tt, GDN_QKV), lambda b, i: (b * nt + i, cq)),
                  pl.BlockSpec((SUBLANES, GDN_QKV), lambda b, i: (jnp.maximum((b * nt + i) * r - 1, 0), cq)),
                  pl.BlockSpec((SHORT_CONV, GDN_QKV), lambda b, i: (0, 0)),
                  pl.BlockSpec((tt, LANES), lambda b, i: (b * nt + i, COL_BA // LANES)),
                  pl.BlockSpec((2, LANES), lambda b, i: (0, 0))],
        out_specs=[pl.BlockSpec((tt, GDN_QK), row), pl.BlockSpec((tt, GDN_QK), row),
                   pl.BlockSpec((tt, GDN_QK), row), pl.BlockSpec((tt, LANES), row)],
        out_shape=[jax.ShapeDtypeStruct((bp * tp, GDN_QK), F32)] * 3 + [jax.ShapeDtypeStruct((bp * tp, LANES), F32)],
        scratch_shapes=[pltpu.VMEM((tt + SUBLANES, GDN_QKV), F32)],
        compiler_params=_cparams(("parallel", "arbitrary")),
        name="gdn_prep_prompt",
    )(act, act, conv_w, act, al)


def _gdn_prep_sample_kernel(ts, x_ref, buf_ref, w_ref, ba_ref, al_ref, q_ref, k_ref, v_ref, gb_ref, s_ref, y_ref, g_ref):
    nb = SHORT_CONV - 1
    y_ref[...] = jnp.zeros_like(y_ref)
    g_ref[...] = jnp.zeros_like(g_ref)
    x = x_ref[...]
    ba = ba_ref[...]
    for bi in range(SAMPLE_BB):
        s_ref[0:nb, :] = buf_ref[bi]
        s_ref[nb:nb + ts, :] = x[bi * ts:(bi + 1) * ts, :]
        y = jnp.zeros((ts, GDN_QKV), F32)
        for j in range(SHORT_CONV):
            y = y + s_ref[j:j + ts, :] * w_ref[j:j + 1, :]
        y_ref[bi * NEW_PAD:bi * NEW_PAD + ts, :] = y
        g_ref[bi * NEW_PAD:bi * NEW_PAD + ts, :] = ba[bi * ts:(bi + 1) * ts, :]
    _gdn_post(y_ref[...], g_ref[...], al_ref[...], q_ref, k_ref, v_ref, gb_ref)
    rowi = lax.broadcasted_iota(jnp.int32, gb_ref.shape, 0) % NEW_PAD
    gb_ref[...] = jnp.where(rowi < ts, gb_ref[...], 0.0)


def _gdn_prep_sample(act, n_p, bs, ts, buf, conv_w, al):
    bb = SAMPLE_BB
    rows = bb * ts
    prow = bb * NEW_PAD
    base = n_p // rows
    nb = SHORT_CONV - 1
    row = lambda i: (i, 0)
    return pl.pallas_call(
        functools.partial(_gdn_prep_sample_kernel, ts),
        grid=(bs // bb,),
        in_specs=[pl.BlockSpec((rows, GDN_QKV), lambda i: (base + i, COL_QKV // GDN_QKV)),
                  pl.BlockSpec((bb, nb, GDN_QKV), lambda i: (i, 0, 0)),
                  pl.BlockSpec((SHORT_CONV, GDN_QKV), lambda i: (0, 0)),
                  pl.BlockSpec((rows, LANES), lambda i: (base + i, COL_BA // LANES)),
                  pl.BlockSpec((2, LANES), lambda i: (0, 0))],
        out_specs=[pl.BlockSpec((prow, GDN_QK), row), pl.BlockSpec((prow, GDN_QK), row),
                   pl.BlockSpec((prow, GDN_QK), row), pl.BlockSpec((prow, LANES), row)],
        out_shape=[jax.ShapeDtypeStruct((bs * NEW_PAD, GDN_QK), F32)] * 3
        + [jax.ShapeDtypeStruct((bs * NEW_PAD, LANES), F32)],
        scratch_shapes=[pltpu.VMEM((2 * SUBLANES, GDN_QKV), F32), pltpu.VMEM((prow, GDN_QKV), F32),
                        pltpu.VMEM((prow, LANES), F32)],
        compiler_params=_cparams(("parallel",)),
        name="gdn_prep_sample",
    )(act, buf, conv_w, act, al)


def _split_bf16(a):
    hi = a.astype(BF16)
    return hi, (a - hi.astype(F32)).astype(BF16)


def _dot_x3(a, b):
    (ah, al), (bh, bl) = a, b
    return _dot(ah, bh) + (_dot(ah, bl) + _dot(al, bh))


def _unit_lower_inverses(nmats, eye, n_double):
    ps = [_split_bf16(-nm) for nm in nmats]
    ts = [eye - nm for nm in nmats]
    for _ in range(n_double):
        ps = [_split_bf16(_dot_x3(p, p)) for p in ps]
        ts = [t + _dot_x3(_split_bf16(t), p) for t, p in zip(ts, ps)]
    return ts


def _gated_rmsnorm(o, ng, z):
    return (o * lax.rsqrt(jnp.mean(o * o, axis=-1, keepdims=True) + EPS) * ng * _silu(z)).astype(BF16)


GDN_CHUNKS_PER_STEP = 2


def _gdn_intra_kernel(q_ref, k_ref, v_ref, gb_ref, u_ref, w_ref, qd_ref, kt_ref, at_ref, gam_ref):
    c = GDN_CHUNK
    ri = lax.broadcasted_iota(jnp.int32, (c, c), 0)
    ci = lax.broadcasted_iota(jnp.int32, (c, c), 1)
    eye = (ci == ri).astype(F32)
    rows = lax.broadcasted_iota(jnp.int32, (c, LANES), 0)
    n_double = int(np.log2(c)) - 1
    units, nmats = [], []
    for cc in range(GDN_CHUNKS_PER_STEP):
        rs = slice(cc * c, (cc + 1) * c)
        gb = gb_ref[rs, :]
        g_cum = gb
        step = 1
        while step < c:
            g_cum = g_cum + jnp.where(rows >= step, pltpu.roll(g_cum, step, axis=0), 0.0)
            step *= 2
        g_cum_t = g_cum.T
        for h in range(GDN_HEADS):
            hs = slice(h * GDN_DK, (h + 1) * GDN_DK)
            q = q_ref[rs, hs]
            k = k_ref[rs, hs]
            beta = gb[:, h:h + 1]
            gc = g_cum[:, GDN_HEADS + h:GDN_HEADS + h + 1]
            gr = g_cum_t[GDN_HEADS + h:GDN_HEADS + h + 1, :]
            g_last = gc[c - 1:c, :]
            dec = jnp.exp(jnp.where(ri >= ci, gc - gr, -jnp.inf))
            kb = k.astype(BF16)
            kk = _dot_nt(kb, kb)
            qk = _dot_nt(q.astype(BF16), kb)
            egc = jnp.exp(gc)
            qd_ref[rs, hs] = (q * egc).astype(BF16)
            kt_ref[rs, hs] = (k * jnp.exp(g_last - gc)).astype(BF16)
            at_ref[rs, h * LANES:h * LANES + c] = (qk * dec).astype(BF16)
            at_ref[rs, h * LANES + c:(h + 1) * LANES] = jnp.zeros((c, LANES - c), BF16)
            gam_ref[cc * GDN_HEADS + h:cc * GDN_HEADS + h + 1, :] = jnp.broadcast_to(jnp.exp(g_last), (1, LANES))
            nmats.append(jnp.where(ri > ci, beta * kk * dec, 0.0))
            units.append((rs, hs, beta, beta * egc))
    tinvs = _unit_lower_inverses(nmats, eye, n_double)
    sols = []
    for (rs, hs, beta, beta_egc), tinv in zip(units, tinvs):
        rhs = jnp.concatenate([beta * v_ref[rs, hs], beta_egc * k_ref[rs, hs]], axis=1)
        sols.append(_dot_x3(_split_bf16(tinv), _split_bf16(rhs)))
    for (rs, hs, _, _), sol in zip(units, sols):
        u_ref[rs, hs] = sol[:, :GDN_DV]
        w_ref[rs, hs] = sol[:, GDN_DV:].astype(BF16)


def _gdn_intra(q, k, v, gb):
    n_p = q.shape[0]
    r = GDN_CHUNKS_PER_STEP * GDN_CHUNK
    row = lambda i: (i, 0)
    wide = pl.BlockSpec((r, GDN_QK), row)
    return pl.pallas_call(
        _gdn_intra_kernel,
        grid=(n_p // r,),
        in_specs=[wide, wide, wide, pl.BlockSpec((r, LANES), row)],
        out_specs=[wide, wide, wide, wide, wide, pl.BlockSpec((GDN_CHUNKS_PER_STEP * GDN_HEADS, LANES), row)],
        out_shape=[jax.ShapeDtypeStruct((n_p, GDN_QK), F32)] + [jax.ShapeDtypeStruct((n_p, GDN_QK), BF16)] * 4
        + [jax.ShapeDtypeStruct((n_p // GDN_CHUNK * GDN_HEADS, LANES), F32)],
        compiler_params=_cparams(("parallel",)),
        name="gdn_intra_chunk",
    )(q, k, v, gb)


def _gdn_scan_kernel(u_ref, w_ref, qd_ref, kt_ref, at_ref, gam_ref, z_ref, s0_ref, ng_ref, o_ref, s_ref):
    c = GDN_CHUNK

    @pl.when(pl.program_id(1) == 0)
    def _():
        s_ref[...] = s0_ref[...]

    heads = range(GDN_HEADS)
    hsl = [slice(h * GDN_DK, (h + 1) * GDN_DK) for h in heads]
    ss = [s_ref[0, h] for h in heads]
    sbs = [s.astype(BF16) for s in ss]
    ws = [_dot(w_ref[:, hsl[h]], sbs[h]) for h in heads]
    os_ = [_dot(qd_ref[:, hsl[h]], sbs[h]) for h in heads]
    ubs = [(u_ref[:, hsl[h]] - ws[h]).astype(BF16) for h in heads]
    upd = [_dot_tn(kt_ref[:, hsl[h]], ubs[h]) for h in heads]
    os_ = [os_[h] + _dot(at_ref[:, h * LANES:h * LANES + c], ubs[h]) for h in heads]
    for h in heads:
        s_ref[0, h] = gam_ref[h:h + 1, :] * ss[h] + upd[h]
        o_ref[:, hsl[h]] = _gated_rmsnorm(os_[h], ng_ref[...], z_ref[:, hsl[h]])


def _gdn_scan(u0, w, qd, kt, at, gam, act, s0, norm_g, nseq, nchunk):
    c = GDN_CHUNK
    row = lambda b, n: (b * nchunk + n, 0)
    wide = pl.BlockSpec((c, GDN_QK), row)
    st = pl.BlockSpec((1, GDN_HEADS, GDN_DK, GDN_DV), lambda b, n: (b, 0, 0, 0))
    return pl.pallas_call(
        _gdn_scan_kernel,
        grid=(nseq, nchunk),
        in_specs=[wide, wide, wide, wide, wide, pl.BlockSpec((GDN_HEADS, LANES), row),
                  pl.BlockSpec((c, GDN_QK), lambda b, n: (b * nchunk + n, COL_Z // GDN_QK)),
                  st, pl.BlockSpec((1, GDN_DV), lambda b, n: (0, 0))],
        out_specs=[wide, st],
        out_shape=[jax.ShapeDtypeStruct((nseq * nchunk * c, GDN_QK), BF16),
                   jax.ShapeDtypeStruct((nseq, GDN_HEADS, GDN_DK, GDN_DV), F32)],
        compiler_params=_cparams(("parallel", "arbitrary")),
        name="gdn_scan_chunks",
    )(u0, w, qd, kt, at, gam, act, s0, norm_g.reshape(1, -1))


GDN_SAMPLE_SEQS = 4


def _gdn_sample_kernel(q_ref, k_ref, v_ref, gb_ref, z_ref, s0_ref, ng_ref, o_ref, s_ref):
    c = NEW_PAD
    r = GDN_HEADS * c
    wide = GDN_HEADS * GDN_DK
    ri = lax.broadcasted_iota(jnp.int32, (r, r), 0)
    ci = lax.broadcasted_iota(jnp.int32, (r, r), 1)
    same = (ri // c) == (ci // c)
    incl = same & (ci <= ri)
    strict = same & (ci < ri)
    eye = (ci == ri).astype(F32)
    incl_f = incl.astype(F32)
    head_of_row = lax.broadcasted_iota(jnp.int32, (r, wide), 0) // c
    head_of_lane = lax.broadcasted_iota(jnp.int32, (r, wide), 1) // GDN_DK
    blk = head_of_row == head_of_lane
    n_double = int(np.log2(c)) - 1

    def stack(x):
        return jnp.concatenate([x[:, h * GDN_DK:(h + 1) * GDN_DK] for h in range(GDN_HEADS)], axis=0)

    def spread(x):
        return jnp.where(blk, jnp.concatenate([x] * GDN_HEADS, axis=1), 0.0)

    seqs = range(GDN_SAMPLE_SEQS)
    rsl = [slice(sq * c, (sq + 1) * c) for sq in seqs]
    pre = []
    for sq in seqs:
        gb = gb_ref[rsl[sq], :]
        q, k = stack(q_ref[rsl[sq], :]), stack(k_ref[rsl[sq], :])
        beta = jnp.concatenate([jnp.broadcast_to(gb[:, h:h + 1], (c, LANES)) for h in range(GDN_HEADS)], axis=0)
        g = jnp.concatenate([jnp.broadcast_to(gb[:, GDN_HEADS + h:GDN_HEADS + h + 1], (c, LANES))
                             for h in range(GDN_HEADS)], axis=0)
        g_cum = _dot(incl_f, g, HIGHEST)
        diff = _dot(incl_f, jnp.where(strict, g[:, :r], 0.0), HIGHEST)
        kb = k.astype(BF16)
        pre.append((q, k, beta, g_cum, diff, _dot_nt(kb, kb), _dot_nt(q.astype(BF16), kb)))
    decs = [jnp.exp(jnp.where(incl, p[4], -jnp.inf)) for p in pre]
    tinvs = _unit_lower_inverses([jnp.where(strict, p[2][:, :r] * p[5] * d, 0.0) for p, d in zip(pre, decs)],
                                 eye, n_double)
    egcs = [jnp.exp(p[3]) for p in pre]
    sols = [_dot_x3(_split_bf16(t), _split_bf16(jnp.concatenate(
        [p[2] * stack(v_ref[rsl[sq], :]), (p[2] * e) * p[1]], axis=1)))
        for sq, (p, e, t) in enumerate(zip(pre, egcs, tinvs))]
    ss = [s0_ref[sq].reshape(wide, GDN_DV) for sq in seqs]
    sbs = [s.astype(BF16) for s in ss]
    ws = [_dot(spread(sol[:, GDN_DV:]).astype(BF16), sb) for sol, sb in zip(sols, sbs)]
    os_ = [_dot(spread(p[0] * e).astype(BF16), sb) for p, e, sb in zip(pre, egcs, sbs)]
    ubs = [(sol[:, :GDN_DV] - w).astype(BF16) for sol, w in zip(sols, ws)]
    os_ = [o + _dot((p[6] * d).astype(BF16), ub) for o, p, d, ub in zip(os_, pre, decs, ubs)]
    for sq in seqs:
        g_cum, k = pre[sq][3], pre[sq][1]
        g_last = jnp.concatenate([jnp.broadcast_to(g_cum[h * c + c - 1:h * c + c, :], (c, LANES))
                                  for h in range(GDN_HEADS)], axis=0)
        gamma = jnp.concatenate([jnp.broadcast_to(jnp.exp(g_cum[h * c + c - 1:h * c + c, :]), (GDN_DK, LANES))
                                 for h in range(GDN_HEADS)], axis=0)
        s_new = gamma * ss[sq] + _dot_tn(spread(k * jnp.exp(g_last - g_cum)).astype(BF16), ubs[sq])
        s_ref[sq] = s_new.reshape(GDN_HEADS, GDN_DK, GDN_DV)
        on = _gated_rmsnorm(os_[sq], ng_ref[...], stack(z_ref[rsl[sq], :]))
        for h in range(GDN_HEADS):
            o_ref[rsl[sq], h * GDN_DK:(h + 1) * GDN_DK] = on[h * c:(h + 1) * c, :]


def _gdn_sample(layer, q, k, v, gb, z, state_all, norm_g):
    bs = state_all.shape[1]
    sq = GDN_SAMPLE_SEQS
    row = lambda i: (i, 0)
    wide = pl.BlockSpec((sq * NEW_PAD, GDN_QK), row)
    st = pl.BlockSpec((sq, GDN_HEADS, GDN_DK, GDN_DV), lambda i: (i, 0, 0, 0))
    return pl.pallas_call(
        _gdn_sample_kernel,
        grid=(bs // sq,),
        in_specs=[wide, wide, wide, pl.BlockSpec((sq * NEW_PAD, LANES), row), wide,
                  pl.BlockSpec((None, sq, GDN_HEADS, GDN_DK, GDN_DV), lambda i: (layer, i, 0, 0, 0)),
                  pl.BlockSpec((1, GDN_DV), lambda i: (0, 0))],
        out_specs=[wide, st],
        out_shape=[jax.ShapeDtypeStruct((bs * NEW_PAD, GDN_QK), BF16),
                   jax.ShapeDtypeStruct((bs, GDN_HEADS, GDN_DK, GDN_DV), F32)],
        compiler_params=_cparams(("parallel",)),
        name="gdn_sample_chunk",
    )(q, k, v, gb, z, state_all, norm_g.reshape(1, -1))


def _merge_kernel(np_tiles, cp_ref, mp_ref, gp_ref, cs_ref, ms_ref, gs_ref, wc_ref, wm_ref, wg_ref,
                  g0_ref, g1_ref, g2_ref, o_ref):
    def merged(c_ref, m_ref, g_ref):
        y = (_sigmoid(g0_ref[...]) * _dot(c_ref[...], wc_ref[...])
             + _sigmoid(g1_ref[...]) * _dot(m_ref[...], wm_ref[...])
             + _sigmoid(g2_ref[...]) * _dot(g_ref[...], wg_ref[...]))
        o_ref[...] = y.astype(BF16)

    @pl.when(pl.program_id(0) < np_tiles)
    def _():
        merged(cp_ref, mp_ref, gp_ref)

    @pl.when(pl.program_id(0) >= np_tiles)
    def _():
        merged(cs_ref, ms_ref, gs_ref)


def _merge(prompt_acts, sample_acts, w_pw, w_mo, w_go, act):
    n_p, n_s = prompt_acts[0].shape[0], sample_acts[0].shape[0]
    tm = _row_tile(int(np.gcd(n_p, n_s)), ROW_TILE)
    np_tiles = n_p // tm
    ap = pl.BlockSpec((tm, CONV_CH), lambda i: (jnp.minimum(i, np_tiles - 1), 0))
    asmp = pl.BlockSpec((tm, CONV_CH), lambda i: (jnp.maximum(i - np_tiles, 0), 0))
    w = pl.BlockSpec((CONV_CH, D_MODEL), lambda i: (0, 0))

    def gate(br):
        return pl.BlockSpec((tm, D_MODEL), lambda i: (i, COL_GATE // D_MODEL + br))

    return pl.pallas_call(
        functools.partial(_merge_kernel, np_tiles),
        grid=((n_p + n_s) // tm,),
        in_specs=[ap, ap, ap, asmp, asmp, asmp, w, w, w, gate(0), gate(1), gate(2)],
        out_specs=pl.BlockSpec((tm, D_MODEL), lambda i: (i, 0)),
        out_shape=jax.ShapeDtypeStruct((n_p + n_s, D_MODEL), BF16),
        compiler_params=_cparams(("parallel",)),
        name="merge_branches",
    )(*prompt_acts, *sample_acts, w_pw, w_mo, w_go, act, act, act)


def _route(x1, w_r, b_r, e_ref, p_ref):
    logits = _dot_x3(_split_bf16(x1), _split_bf16(w_r)) + b_r
    lane = lax.broadcasted_iota(jnp.int32, logits.shape, 1).astype(F32)
    big = jnp.float32(1e9)
    lg = jnp.where(lane < N_GROUPS, logits, -jnp.inf)
    mg = jnp.max(lg, axis=-1, keepdims=True)
    g_val = 1.0 / jnp.sum(jnp.exp(lg - mg), axis=-1, keepdims=True)
    g_idx = jnp.min(jnp.where(lg == mg, lane, big), axis=-1, keepdims=True)
    lo = N_GROUPS + g_idx * EXPERTS_PER_GROUP
    le = jnp.where((lane >= lo) & (lane < lo + EXPERTS_PER_GROUP), logits, -jnp.inf)
    m1 = jnp.max(le, axis=-1, keepdims=True)
    i1 = jnp.min(jnp.where(le == m1, lane, big), axis=-1, keepdims=True)
    le2 = jnp.where(lane == i1, -jnp.inf, le)
    m2 = jnp.max(le2, axis=-1, keepdims=True)
    i2 = jnp.min(jnp.where(le2 == m2, lane, big), axis=-1, keepdims=True)
    se = jnp.sum(jnp.exp(le - m1), axis=-1, keepdims=True)
    p1 = 1.0 / se
    p2 = jnp.exp(m2 - m1) / se
    w1 = g_val * (p1 / (p1 + p2))
    w2 = g_val * (p2 / (p1 + p2))
    e_ref[...] = jnp.where(lane == 0, i1 - N_GROUPS, jnp.where(lane == 1, i2 - N_GROUPS, 0.0)).astype(jnp.int32)
    p_ref[...] = jnp.where(lane == 0, w1, jnp.where(lane == 1, w2, 0.0))


def _out_ln_kernel(alpha, m_ref, w_ref, x_ref, g_ref, b_ref, wr_ref, br_ref, o_ref, e_ref, p_ref):
    y = alpha * x_ref[...] + _dot(m_ref[...], w_ref[...])
    x1 = _ln_rows(y, g_ref[...], b_ref[...])
    o_ref[...] = x1
    _route(x1, wr_ref[...], br_ref[...], e_ref, p_ref)


def _out_ln_route(merged, w_out, xt, ln_g, ln_b, alpha, w_r, b_r):
    n = xt.shape[0]
    tm = ROW_TILE
    vec = pl.BlockSpec((1, D_MODEL), lambda i: (0, 0))
    small = pl.BlockSpec((tm, LANES), lambda i: (i, 0))
    return pl.pallas_call(
        functools.partial(_out_ln_kernel, alpha),
        grid=(n // tm,),
        in_specs=[pl.BlockSpec((tm, D_MODEL), lambda i: (i, 0)),
                  pl.BlockSpec((D_MODEL, D_MODEL), lambda i: (0, 0)),
                  pl.BlockSpec((tm, D_MODEL), lambda i: (i, 0)), vec, vec,
                  pl.BlockSpec((D_MODEL, LANES), lambda i: (0, 0)),
                  pl.BlockSpec((1, LANES), lambda i: (0, 0))],
        out_specs=[pl.BlockSpec((tm, D_MODEL), lambda i: (i, 0)), small, small],
        out_shape=[jax.ShapeDtypeStruct((n, D_MODEL), F32), jax.ShapeDtypeStruct((n, LANES), jnp.int32),
                   jax.ShapeDtypeStruct((n, LANES), F32)],
        compiler_params=_cparams(("parallel",)),
        name="out_proj_ln1_route",
    )(merged, w_out, xt, ln_g.reshape(1, -1), ln_b.reshape(1, -1), w_r, b_r)


def _gather_rows(src_hbm, dst, sem, idx_ref, base, count):
    def body(r, carry):
        pltpu.make_async_copy(src_hbm.at[pl.ds(idx_ref[base + r], 1)], dst.at[pl.ds(r, 1)], sem).start()
        return carry
    lax.fori_loop(0, count, body, 0, unroll=8)


def _wait_rows(src_hbm, dst, sem, count):
    pltpu.make_async_copy(src_hbm.at[pl.ds(0, count)], dst.at[pl.ds(0, count)], sem).wait()


def _moe_kernel(te_ref, src_ref, nu_ref, x_hbm, wg_ref, wu_ref, wd_ref, y_ref,
                xbuf, sem, wgb, wub, wdb):
    t = pl.program_id(0)
    nt = pl.num_programs(0)
    n_used = nu_ref[0]
    slot = t % MOE_SLOTS
    te = MOE_TILE
    ahead = MOE_SLOTS - 1

    @pl.when(t == 0)
    def _():
        for a in range(ahead):
            @pl.when(a < jnp.maximum(n_used, 1))
            def _():
                _gather_rows(x_hbm, xbuf.at[a], sem.at[a], src_ref, a * te, te)

    @pl.when(t + ahead < n_used)
    def _():
        nxt = (t + ahead) % MOE_SLOTS
        _gather_rows(x_hbm, xbuf.at[nxt], sem.at[nxt], src_ref, (t + ahead) * te, te)

    changed = jnp.logical_or(t == 0, te_ref[t] != te_ref[jnp.maximum(t - 1, 0)])

    @pl.when(jnp.logical_and(changed, t < n_used))
    def _():
        wgb[...] = wg_ref[...].astype(BF16)
        wub[...] = wu_ref[...].astype(BF16)
        wdb[...] = wd_ref[...].astype(BF16)

    @pl.when(jnp.logical_or(t < n_used, t == 0))
    def _():
        _wait_rows(x_hbm, xbuf.at[slot], sem.at[slot], te)

    @pl.when(t < n_used)
    def _():
        xb = xbuf[slot].astype(BF16)
        h = _silu(_dot(xb, wgb[...])) * _dot(xb, wub[...])
        y_ref[...] = _dot(h.astype(BF16), wdb[...])

    @pl.when(t >= n_used)
    def _():
        y_ref[...] = jnp.zeros_like(y_ref)


def _moe_experts(layer, tile_expert, row_src, n_used, x1, w_gate, w_up, w_down):
    n_tiles = tile_expert.shape[0]
    te = MOE_TILE
    grid_spec = pltpu.PrefetchScalarGridSpec(
        num_scalar_prefetch=3,
        grid=(n_tiles,),
        in_specs=[pl.BlockSpec(memory_space=pl.ANY),
                  pl.BlockSpec((None, None, D_MODEL, EXPERT_FF), lambda t, te_r, s_r, n_r: (layer, te_r[t], 0, 0)),
                  pl.BlockSpec((None, None, D_MODEL, EXPERT_FF), lambda t, te_r, s_r, n_r: (layer, te_r[t], 0, 0)),
                  pl.BlockSpec((None, None, EXPERT_FF, D_MODEL), lambda t, te_r, s_r, n_r: (layer, te_r[t], 0, 0))],
        out_specs=pl.BlockSpec((te, D_MODEL), lambda t, te_r, s_r, n_r: (t, 0)),
        scratch_shapes=[pltpu.VMEM((MOE_SLOTS, te, D_MODEL), F32), pltpu.SemaphoreType.DMA((MOE_SLOTS,)),
                        pltpu.VMEM((D_MODEL, EXPERT_FF), BF16), pltpu.VMEM((D_MODEL, EXPERT_FF), BF16),
                        pltpu.VMEM((EXPERT_FF, D_MODEL), BF16)],
    )
    return pl.pallas_call(
        _moe_kernel,
        grid_spec=grid_spec,
        out_shape=jax.ShapeDtypeStruct((n_tiles * te, D_MODEL), F32),
        compiler_params=_cparams(("arbitrary",)),
        name="moe_experts",
    )(tile_expert, row_src, n_used, x1, w_gate, w_up, w_down)


def _combine_ln_kernel(alpha, pos_ref, y_hbm, x_ref, p_ref, g_ref, b_ref, o_ref, ob_ref, ybuf, sem):
    t = pl.program_id(0)
    nt = pl.num_programs(0)
    tm = x_ref.shape[0]
    slot = t % 2

    @pl.when(t == 0)
    def _():
        _gather_rows(y_hbm, ybuf.at[0], sem.at[0], pos_ref, 0, 2 * tm)

    @pl.when(t + 1 < nt)
    def _():
        _gather_rows(y_hbm, ybuf.at[1 - slot], sem.at[1 - slot], pos_ref, (t + 1) * 2 * tm, 2 * tm)

    _wait_rows(y_hbm, ybuf.at[slot], sem.at[slot], 2 * tm)
    p = p_ref[...]
    y = alpha * x_ref[...] + p[:, 0:1] * ybuf[slot, 0:tm, :] + p[:, 1:2] * ybuf[slot, tm:2 * tm, :]
    x2 = _ln_rows(y, g_ref[...], b_ref[...])
    o_ref[...] = x2
    ob_ref[...] = x2.astype(BF16)


def _combine_ln(pos, y_sorted, x1, wts, ln_g, ln_b, alpha):
    n = x1.shape[0]
    tm = ROW_TILE
    vec = pl.BlockSpec((1, D_MODEL), lambda i, p: (0, 0))
    grid_spec = pltpu.PrefetchScalarGridSpec(
        num_scalar_prefetch=1,
        grid=(n // tm,),
        in_specs=[pl.BlockSpec(memory_space=pl.ANY),
                  pl.BlockSpec((tm, D_MODEL), lambda i, p: (i, 0)),
                  pl.BlockSpec((tm, LANES), lambda i, p: (i, 0)), vec, vec],
        out_specs=[pl.BlockSpec((tm, D_MODEL), lambda i, p: (i, 0)), pl.BlockSpec((tm, D_MODEL), lambda i, p: (i, 0))],
        scratch_shapes=[pltpu.VMEM((2, 2 * tm, D_MODEL), F32), pltpu.SemaphoreType.DMA((2,))],
    )
    return pl.pallas_call(
        functools.partial(_combine_ln_kernel, alpha),
        grid_spec=grid_spec,
        out_shape=[jax.ShapeDtypeStruct((n, D_MODEL), F32), jax.ShapeDtypeStruct((n, D_MODEL), BF16)],
        compiler_params=_cparams(("arbitrary",)),
        name="moe_combine_ln2",
    )(pos, y_sorted, x1, wts, ln_g.reshape(1, -1), ln_b.reshape(1, -1))


def _routing_tables(eid, n):
    te = MOE_TILE
    n_tiles = (2 * n) // te + N_EXPERTS
    e = eid.reshape(-1)
    onehot = (e[:, None] == jnp.arange(N_EXPERTS, dtype=jnp.int32)[None, :]).astype(jnp.int32)
    rank = jnp.sum((jnp.cumsum(onehot, axis=0) - onehot) * onehot, axis=1)
    counts = jnp.sum(onehot, axis=0)
    tiles_per = (counts + te - 1) // te
    tile_end = jnp.cumsum(tiles_per)
    tile_start = tile_end - tiles_per
    dest = tile_start[e] * te + rank
    n_used = tile_end[-1:].astype(jnp.int32)
    tile_expert = jnp.minimum(jnp.searchsorted(tile_end, jnp.arange(n_tiles, dtype=jnp.int32), side="right"),
                              N_EXPERTS - 1).astype(jnp.int32)
    token = jnp.arange(2 * n, dtype=jnp.int32) // 2
    row_src = jnp.zeros((n_tiles * te,), jnp.int32).at[dest].set(token)
    tm = ROW_TILE
    pos = dest.reshape(n // tm, tm, 2).transpose(0, 2, 1).reshape(-1).astype(jnp.int32)
    return tile_expert, row_src, n_used, pos


def _rot_cols(w):
    half = ROPE_DIM // 2
    return jnp.concatenate([-w[..., half:], w[..., :half]], axis=-1)


def _pack_w_in_t(w_in):
    d = w_in.shape[0]
    wt = jnp.swapaxes(w_in, 0, 1)
    offs = np.cumsum([0, 2 * CONV_CH, Q_LORA, KV_LORA, ROPE_DIM, GDN_QKV, GDN_HEADS * GDN_DV, GDN_HEADS, GDN_HEADS,
                      N_BRANCHES * D_MODEL])
    seg = [wt[offs[i]:offs[i + 1]] for i in range(9)]
    glu, cq, ckv, kr, qkv, z, b_raw, a_raw, gate = seg
    half = ROPE_DIM // 2
    kr_rot = jnp.concatenate([-kr[half:], kr[:half]], axis=0)
    ba = jnp.concatenate([b_raw, a_raw, jnp.zeros((LANES - 2 * GDN_HEADS, d), w_in.dtype)], axis=0)
    rows = [gate, qkv, glu[:CONV_CH], glu[CONV_CH:], z, cq, ckv, kr, kr_rot, ba,
            jnp.zeros((N_PACK - COL_BA - LANES, d), w_in.dtype)]
    return jnp.concatenate(rows, axis=0)


def _pack_w_uq(w_uq):
    w = w_uq.reshape(Q_LORA, MLA_HEADS, NOPE_DIM + ROPE_DIM)
    nope, rope = w[..., :NOPE_DIM], w[..., NOPE_DIM:]
    zpad = jnp.zeros((Q_LORA, MLA_HEADS, HEAD_QK - NOPE_DIM - ROPE_DIM), w.dtype)
    wa = jnp.concatenate([nope, rope, zpad], axis=-1)
    wb = jnp.concatenate([jnp.zeros_like(nope), _rot_cols(rope), zpad], axis=-1)
    return wa.reshape(Q_LORA, -1).astype(BF16), wb.reshape(Q_LORA, -1).astype(BF16)


def _rope_tables(pos, scale):
    half = ROPE_DIM // 2
    inv_freq = ROPE_THETA ** (-jnp.arange(half, dtype=F32) / half)
    ang = pos.astype(F32)[:, None] * inv_freq[None, :]
    cos = jnp.concatenate([jnp.cos(ang), jnp.cos(ang)], axis=-1)
    sin = jnp.concatenate([jnp.sin(ang), jnp.sin(ang)], axis=-1)
    n = pos.shape[0]
    zpad = jnp.zeros((n, HEAD_QK - NOPE_DIM - ROPE_DIM), F32)
    tab_a = jnp.concatenate([jnp.full((n, NOPE_DIM), scale, F32), scale * cos, zpad], axis=-1)
    tab_b = jnp.concatenate([jnp.zeros((n, NOPE_DIM), F32), scale * sin, zpad], axis=-1)
    tab_cs = jnp.concatenate([cos, sin], axis=-1)
    return tab_a, tab_b, tab_cs


def kernel(x_prompt, x_sample, cache_ckv, cache_krope, state_conf_conv, state_gdn_conv, state_gdn, page_table, w_in, conv_dw, conv_dw_b, conv_ln_g, conv_ln_b, conv_pw, mla_q_norm, mla_w_uq, mla_kv_norm, mla_w_uk, mla_w_uv, mla_w_o, gdn_conv_w, gdn_a_log, gdn_dt_bias, gdn_norm, gdn_w_o, w_out, ln1_g, ln1_b, router_group_w, router_group_b, router_expert_w, router_expert_b, moe_w_gate, moe_w_up, moe_w_down, ln2_g, ln2_b):
    bp, tp, d = x_prompt.shape
    bs, ts, _ = x_sample.shape
    depth = w_in.shape[0]
    n_pages, page = page_table.shape[1], cache_ckv.shape[2]
    past_len = n_pages * page
    n_p, n_s = bp * tp, bs * ts
    n = n_p + n_s
    assert d == D_MODEL and w_in.shape[2] == 2 * CONV_CH + Q_LORA + KV_LORA + ROPE_DIM + GDN_QKV + GDN_HEADS * GDN_DV \
        + 2 * GDN_HEADS + N_BRANCHES * D_MODEL
    assert tp % ATT_TILE == 0 and tp % GDN_CHUNK == 0 and n % ROW_TILE == 0 and n_p % ROW_TILE == 0
    assert ts <= NEW_PAD and bs % SAMPLE_BB == 0 and n_p % (SAMPLE_BB * ts) == 0 and n_pages % PAGES_PER_STEP == 0
    assert (2 * n) % MOE_TILE == 0 and bs % GDN_SAMPLE_SEQS == 0 and tp % (GDN_CHUNKS_PER_STEP * GDN_CHUNK) == 0
    alpha = float((2.0 * depth) ** 0.25)
    scale = float((NOPE_DIM + ROPE_DIM) ** -0.5)

    pos = jnp.concatenate([jnp.tile(jnp.arange(tp, dtype=jnp.int32), bp),
                           jnp.tile(past_len + jnp.arange(ts, dtype=jnp.int32), bs)])
    tab_a, tab_b, tab_cs = _rope_tables(pos, scale)
    e_place = jnp.zeros((LANES, MLA_HEADS, HEAD_QK), F32)
    e_place = e_place.at[jnp.arange(ROPE_DIM)[:, None], jnp.arange(MLA_HEADS)[None, :],
                         NOPE_DIM + jnp.arange(ROPE_DIM)[:, None]].set(1.0).reshape(LANES, -1).astype(BF16)
    zeros_state = jnp.zeros((bp, GDN_HEADS, GDN_DK, GDN_DV), F32)
    cache_krope_t = jnp.swapaxes(cache_krope, 2, 3)

    xt = jnp.concatenate([x_prompt.reshape(n_p, d), x_sample.reshape(n_s, d)], axis=0)
    xb = xt.astype(BF16)
    outs = [[] for _ in range(10)]
    for l in range(depth):
        w_pack = _pack_w_in_t(w_in[l])
        w_qa, w_qb = _pack_w_uq(mla_w_uq[l])
        w_uk = mla_w_uk[l]
        w_uk_pad = jnp.concatenate([w_uk, jnp.zeros((KV_LORA, MLA_HEADS, HEAD_QK - NOPE_DIM), F32)],
                                   axis=-1).reshape(KV_LORA, -1).astype(BF16)
        w_uk_t = jnp.transpose(w_uk, (1, 2, 0)).astype(BF16)
        w_uv = mla_w_uv[l].reshape(KV_LORA, -1).astype(BF16)
        al = jnp.zeros((2, LANES), F32).at[0, GDN_HEADS:2 * GDN_HEADS].set(gdn_a_log[l]) \
            .at[1, GDN_HEADS:2 * GDN_HEADS].set(gdn_dt_bias[l])
        w_r = jnp.concatenate([router_group_w[l], router_expert_w[l],
                               jnp.zeros((d, LANES - N_GROUPS - N_EXPERTS), F32)], axis=1)
        b_r = jnp.concatenate([router_group_b[l], router_expert_b[l],
                               jnp.zeros((LANES - N_GROUPS - N_EXPERTS,), F32)]).reshape(1, LANES)

        act = _inproj(xb, w_pack)

        c_p, u_p = _conf_prompt(act, bp, tp, conv_dw[l], conv_dw_b[l], conv_ln_g[l], conv_ln_b[l])
        c_s, conf_buf_s = _conf_sample(act, n_p, bs, ts, state_conf_conv[l], conv_dw[l], conv_dw_b[l],
                                       conv_ln_g[l], conv_ln_b[l])
        conf_buf_p = u_p.reshape(bp, tp, CONV_CH)[:, tp - (CONV_WIDTH - 1):]

        q = _mla_q(act, mla_q_norm[l], w_qa, w_qb, tab_a, tab_b)
        ckv, krope, k_full, v_full = _mla_kv(act, mla_kv_norm[l], tab_cs, w_uk_pad, w_uv, e_place)
        o_p = _flash_prompt(q, k_full, v_full, bp, tp)
        q_abs = _q_absorb(q[n_p:], w_uk_t).reshape(bs, ts * MLA_HEADS, Q_ABS)
        ckv_s = ckv[n_p:].reshape(bs, ts, KV_LORA)
        kr_s = krope[n_p:].reshape(bs, ts, ROPE_DIM)
        pad = ((0, 0), (0, NEW_PAD - ts), (0, 0))
        kr_new_t = jnp.pad(jnp.swapaxes(kr_s, 1, 2), ((0, 0), (0, 0), (0, LANES - ts)))
        o_lat = _paged_attention(l, page_table, q_abs, jnp.pad(ckv_s, pad), kr_new_t,
                                 cache_ckv, cache_krope_t, ts)
        o_s = _value_up(o_lat.reshape(n_s, MLA_HEADS * KV_LORA), w_uv)

        qg_p, kg_p, vg_p, gb_p = _gdn_prep_prompt(act, bp, tp, gdn_conv_w[l], al)
        og_p, st_p = _gdn_scan(*_gdn_intra(qg_p, kg_p, vg_p, gb_p), act, zeros_state, gdn_norm[l],
                               bp, tp // GDN_CHUNK)
        qg_s, kg_s, vg_s, gb_s = _gdn_prep_sample(act, n_p, bs, ts, state_gdn_conv[l], gdn_conv_w[l], al)
        z_s = jnp.pad(act[n_p:, COL_Z:COL_Z + GDN_QK].reshape(bs, ts, GDN_QK), pad).reshape(bs * NEW_PAD, GDN_QK)
        og_s, st_s = _gdn_sample(l, qg_s, kg_s, vg_s, gb_s, z_s, state_gdn, gdn_norm[l])
        og_s = og_s.reshape(bs, NEW_PAD, GDN_QK)[:, :ts].reshape(n_s, GDN_QK)
        nb = SHORT_CONV - 1
        gconv_p = jnp.stack([lax.slice(act, ((b + 1) * tp - nb, COL_QKV), ((b + 1) * tp, COL_QKV + GDN_QKV))
                             for b in range(bp)])
        gconv_s = lax.slice(act, (n_p, COL_QKV), (n, COL_QKV + GDN_QKV)).reshape(bs, ts, GDN_QKV)[:, max(ts - nb, 0):]
        if ts < nb:
            gconv_s = jnp.concatenate([state_gdn_conv[l][:, ts:], gconv_s], axis=1)

        merged = _merge((c_p, o_p, og_p), (c_s, o_s, og_s), conv_pw[l].astype(BF16), mla_w_o[l].astype(BF16),
                        gdn_w_o[l].astype(BF16), act)
        x1, eid, wts = _out_ln_route(merged, w_out[l].astype(BF16), xt, ln1_g[l], ln1_b[l], alpha, w_r, b_r)

        tile_expert, row_src, n_used, pos_tab = _routing_tables(eid[:, :2], n)
        y_sorted = _moe_experts(l, tile_expert, row_src, n_used, x1, moe_w_gate, moe_w_up, moe_w_down)
        xt, xb = _combine_ln(pos_tab, y_sorted, x1, wts, ln2_g[l], ln2_b[l], alpha)

        new = [ckv[:n_p].reshape(bp, tp, KV_LORA), krope[:n_p].reshape(bp, tp, ROPE_DIM), ckv_s, kr_s,
               conf_buf_p, conf_buf_s, gconv_p, gconv_s, st_p, st_s]
        for lst, val in zip(outs, new):
            lst.append(val)

    return (xt[:n_p].reshape(bp, tp, d), xt[n_p:].reshape(bs, ts, d)) + tuple(jnp.stack(o) for o in outs)
```
